```python
import math
import jax, jax.numpy as jnp
from jax import lax
import numpy as np

D_MODEL = 1024
BATCH = 16
SEQ = 2048
DEPTH = 2

CONV_CH = 512
CONV_GROUPS = 8
CONV_WIDTH = 31
ATT_HEADS = 8
ATT_HEAD_DIM = 64
IDX_HEADS = 8
IDX_HEAD_DIM = 64
IDX_TOPK = 256
Q_BLOCK = 128
GLA_HEADS = 4
GLA_DK = 128
GLA_DV = 256
GLA_GATE_RANK = 16
GLA_TAU = 16.0
GLA_CHUNK = 64
ROPE_THETA = 500000.0
ROPE_FRAC = 4
D_FF = -(-8 * D_MODEL // (3 * 256)) * 256
NORM_EPS = 1e-6

AB_SIZES = (2 * CONV_CH, ATT_HEADS * ATT_HEAD_DIM, ATT_HEAD_DIM, ATT_HEAD_DIM,
            IDX_HEADS * IDX_HEAD_DIM, IDX_HEAD_DIM, IDX_HEADS)
AB_IN = sum(AB_SIZES)
AB_OUT = CONV_CH + ATT_HEADS * ATT_HEAD_DIM
C_SIZES = (GLA_HEADS * GLA_DK, GLA_HEADS * GLA_DK, GLA_HEADS * GLA_DV,
           GLA_HEADS * GLA_DV, GLA_GATE_RANK)
C_IN = sum(C_SIZES)
C_OUT = GLA_HEADS * GLA_DV

kernel_name = "hybrid_conv_dsa_gla_trunk"


def split_cols(u, sizes):
    outs, off = [], 0
    for s in sizes:
        outs.append(u[..., off:off + s])
        off += s
    return outs


def rms_norm(x, g):
    xf = x.astype(jnp.float32)
    y = xf * lax.rsqrt(jnp.mean(xf * xf, axis=-1, keepdims=True) + NORM_EPS)
    return (y * g.astype(jnp.float32)).astype(x.dtype)


def rope_tables(seq, head_dim):
    rot = head_dim // ROPE_FRAC
    inv = ROPE_THETA ** (-jnp.arange(0, rot, 2, dtype=jnp.float32) / rot)
    ang = jnp.arange(seq, dtype=jnp.float32)[:, None] * inv[None, :]
    return jnp.cos(ang), jnp.sin(ang)


def apply_rope(x, cos, sin):
    r = cos.shape[-1]
    xf = x.astype(jnp.float32)
    x1, x2, xp = xf[..., :r], xf[..., r:2 * r], xf[..., 2 * r:]
    out = jnp.concatenate([x1 * cos - x2 * sin, x2 * cos + x1 * sin, xp], axis=-1)
    return out.astype(x.dtype)


def conv_module(u, conv_w, conv_b, ln_g, ln_b):
    a, gate = u[..., :CONV_CH], u[..., CONV_CH:]
    h = a * jax.nn.sigmoid(gate)
    h = lax.conv_general_dilated(h, conv_w, window_strides=(1,),
                                 padding=[(CONV_WIDTH - 1, 0)],
                                 dimension_numbers=("NWC", "WIO", "NWC"),
                                 feature_group_count=CONV_CH) + conv_b
    hf = h.astype(jnp.float32)
    mu = jnp.mean(hf, axis=-1, keepdims=True)
    var = jnp.mean(jnp.square(hf - mu), axis=-1, keepdims=True)
    hf = (hf - mu) * lax.rsqrt(var + NORM_EPS) * ln_g.astype(jnp.float32) + ln_b.astype(jnp.float32)
    return jax.nn.silu(hf).astype(u.dtype)


def dsa_attention(q, k, v, iq, ik, iw, cos, sin):
    b_, s_, h_, dh = q.shape
    n_sel = min(IDX_TOPK, s_ // 4)
    q = apply_rope(q, cos[:, None, :], sin[:, None, :])
    k = apply_rope(k, cos, sin)
    iq = apply_rope(iq, cos[:, None, :], sin[:, None, :])
    ik = apply_rope(ik, cos, sin)
    nb = s_ // Q_BLOCK

    def to_blocks(t):
        return jnp.moveaxis(t.reshape(b_, nb, Q_BLOCK, *t.shape[2:]), 1, 0)

    key_pos = jnp.arange(s_, dtype=jnp.int32)
    q_pos = key_pos.reshape(nb, Q_BLOCK)
    w_scale = (IDX_HEADS ** -0.5) * (IDX_HEAD_DIM ** -0.5)
    att_scale = dh ** -0.5

    def block(args):
        qb, iqb, iwb, tpos = args
        logits = jnp.einsum("bqhd,bsd->bqsh", iqb, ik, preferred_element_type=jnp.float32)
        score = jnp.einsum("bqsh,bqh->bqs", jax.nn.relu(logits),
                           iwb.astype(jnp.float32)) * w_scale
        causal = key_pos[None, :] <= tpos[:, None]
        score = jnp.where(causal[None], score, -jnp.inf)
        _, idx = lax.top_k(score, n_sel)
        valid = idx <= tpos[None, :, None]
        kg = jax.vmap(lambda kk, ii: kk[ii])(k, idx)
        vg = jax.vmap(lambda vv, ii: vv[ii])(v, idx)
        s = jnp.einsum("bqhd,bqkd->bhqk", qb, kg, preferred_element_type=jnp.float32) * att_scale
        s = jnp.where(valid[:, None], s, -jnp.inf)
        p = jax.nn.softmax(s, axis=-1)
        return jnp.einsum("bhqk,bqkd->bqhd", p.astype(vg.dtype), vg)

    out = lax.map(block, (to_blocks(q), to_blocks(iq), to_blocks(iw), q_pos))
    return jnp.moveaxis(out, 0, 1).reshape(b_, s_, h_ * dh)


def gla(q, k, v, r, gate_lr, g2_w, g2_b, onorm_g):
    b_, s_, h_, dk = q.shape
    dv = v.shape[-1]
    log_a = jax.nn.log_sigmoid((gate_lr @ g2_w + g2_b).astype(jnp.float32)) / GLA_TAU
    log_a = log_a.reshape(b_, s_, h_, dk)
    nc, c = s_ // GLA_CHUNK, GLA_CHUNK

    def chunks(t):
        return t.astype(jnp.float32).reshape(b_, nc, c, h_, t.shape[-1]).transpose(1, 0, 3, 2, 4)

    qc = chunks(q) * (dk ** -0.5)
    kc, vc, lc = chunks(k), chunks(v), chunks(log_a)
    bcum = jnp.cumsum(lc, axis=3)
    b_last = bcum[..., -1:, :]
    q_t = qc * jnp.exp(bcum)
    k_t = kc * jnp.exp(-bcum)
    k_s = kc * jnp.exp(b_last - bcum)
    mask = jnp.tril(jnp.ones((c, c), jnp.float32))
    att = jnp.einsum("nbhid,nbhjd->nbhij", q_t, k_t) * mask
    o_intra = jnp.einsum("nbhij,nbhjv->nbhiv", att, vc)

    def step(state, inp):
        q_n, k_n, v_n, decay = inp
        o = jnp.einsum("bhid,bhdv->bhiv", q_n, state)
        state = state * decay[:, :, 0, :, None] + jnp.einsum("bhjd,bhjv->bhdv", k_n, v_n)
        return state, o

    s0 = jnp.zeros((b_, h_, dk, dv), jnp.float32)
    _, o_inter = lax.scan(step, s0, (q_t, k_s, vc, jnp.exp(b_last)))
    o = (o_intra + o_inter).transpose(1, 0, 3, 2, 4).reshape(b_, s_, h_, dv)
    o = o * lax.rsqrt(jnp.mean(o * o, axis=-1, keepdims=True) + NORM_EPS) \
        * onorm_g.astype(jnp.float32).reshape(h_, dv)
    o = o.astype(r.dtype) * jax.nn.silu(r)
    return o.reshape(b_, s_, h_ * dv)


def setup_inputs(seed: int = 0) -> dict:
    key = jax.random.key(seed)
    ks = iter(jax.random.split(key, 32))
    n_even = (DEPTH + 1) // 2
    n_odd = DEPTH // 2

    def nrm(shape, scale):
        return jax.random.normal(next(ks), shape, jnp.float32) * scale

    def gain(shape):
        return 1.0 + nrm(shape, 0.02)

    res_scale = (2.0 * DEPTH) ** -0.5
    return {
        "x": nrm((BATCH, SEQ, D_MODEL), 1.0),
        "norm_mix_g": gain((DEPTH, D_MODEL)),
        "ab_w_in": nrm((n_even, D_MODEL, AB_IN), D_MODEL ** -0.5),
        "ab_conv_w": nrm((n_even, CONV_WIDTH, 1, CONV_CH), CONV_WIDTH ** -0.5),
        "ab_conv_b": nrm((n_even, CONV_CH), 0.02),
        "ab_ln_g": gain((n_even, CONV_CH)),
        "ab_ln_b": nrm((n_even, CONV_CH), 0.02),
        "ab_w_out": nrm((n_even, AB_OUT, D_MODEL), AB_OUT ** -0.5 * res_scale),
        "c_w_in": nrm((n_odd, D_MODEL, C_IN), D_MODEL ** -0.5),
        "c_gate_w": nrm((n_odd, GLA_GATE_RANK, GLA_HEADS * GLA_DK), GLA_GATE_RANK ** -0.5),
        "c_gate_b": nrm((n_odd, GLA_HEADS * GLA_DK), 0.02),
        "c_onorm_g": gain((n_odd, GLA_HEADS * GLA_DV)),
        "c_w_out": nrm((n_odd, C_OUT, D_MODEL), C_OUT ** -0.5 * res_scale),
        "norm_ffn_g": gain((DEPTH, D_MODEL)),
        "ffn_w_gate": nrm((DEPTH, D_MODEL, D_FF), D_MODEL ** -0.5),
        "ffn_w_up": nrm((DEPTH, D_MODEL, D_FF), D_MODEL ** -0.5),
        "ffn_w_down": nrm((DEPTH, D_FF, D_MODEL), D_FF ** -0.5 * res_scale),
        "final_norm_g": gain((D_MODEL,)),
    }


def reference(x, norm_mix_g, ab_w_in, ab_conv_w, ab_conv_b, ab_ln_g, ab_ln_b, ab_w_out,
              c_w_in, c_gate_w, c_gate_b, c_onorm_g, c_w_out,
              norm_ffn_g, ffn_w_gate, ffn_w_up, ffn_w_down, final_norm_g):
    b_, s_, _ = x.shape
    cos, sin = rope_tables(s_, ATT_HEAD_DIM)
    h = x
    for layer in range(DEPTH):
        hn = rms_norm(h, norm_mix_g[layer])
        if layer % 2 == 0:
            i = layer // 2
            u = hn @ ab_w_in[i]
            ua, q, k, v, iq, ik, iw = split_cols(u, AB_SIZES)
            ya = conv_module(ua, ab_conv_w[i], ab_conv_b[i], ab_ln_g[i], ab_ln_b[i])
            yb = dsa_attention(q.reshape(b_, s_, ATT_HEADS, ATT_HEAD_DIM), k, v,
                               iq.reshape(b_, s_, IDX_HEADS, IDX_HEAD_DIM), ik, iw, cos, sin)
            y = jnp.concatenate([ya, yb], axis=-1) @ ab_w_out[i]
        else:
            i = layer // 2
            u = hn @ c_w_in[i]
            q, k, v, r, glr = split_cols(u, C_SIZES)
            yc = gla(q.reshape(b_, s_, GLA_HEADS, GLA_DK), k.reshape(b_, s_, GLA_HEADS, GLA_DK),
                     v.reshape(b_, s_, GLA_HEADS, GLA_DV), r.reshape(b_, s_, GLA_HEADS, GLA_DV),
                     glr, c_gate_w[i], c_gate_b[i], c_onorm_g[i])
            y = yc @ c_w_out[i]
        h = h + y
        hn = rms_norm(h, norm_ffn_g[layer])
        h = h + (jax.nn.silu(hn @ ffn_w_gate[layer]) * (hn @ ffn_w_up[layer])) @ ffn_w_down[layer]
    return rms_norm(h, final_norm_g)
```

```python
import functools

import jax
import jax.numpy as jnp
from jax import lax
from jax.experimental import pallas as pl
from jax.experimental.pallas import tpu as pltpu

F32 = jnp.float32
BF16 = jnp.bfloat16

D_MODEL = 1024
CONV_CH = 512
CONV_WIDTH = 31
ATT_HEADS = 8
HEAD_DIM = 64
IDX_HEADS = 8
IDX_TOPK = 256
GLA_HEADS = 4
GLA_DK = 128
GLA_DV = 256
GLA_GATE_RANK = 16
GLA_TAU = 16.0
GLA_CHUNK = 64
ROPE_THETA = 500000.0
ROPE_ROT = HEAD_DIM // 4
D_FF = 2816
NORM_EPS = 1e-6

LANES = 128
SUBLANES = 8
VMEM_LIMIT = 56 * 1024 * 1024

PROJ_TM = 512
FFN_TM = 512
FFN_FC = 256
CONV_TS = 256
CONV_RC = 32
CONV_HALO = 32
DSA_TQ = 256
GLA_TB = 512
GLA_SUB = 256
BF16_ROWS = 16
TOPK_LEVELS = 4
TOPK_MAX_FIX_ROUNDS = 400
TIE_CHUNK = 256


def _sigmoid(x):
    return 1.0 / (1.0 + jnp.exp(-x))


def _rms(x, g):
    ms = jnp.mean(x * x, axis=-1, keepdims=True)
    return x * lax.rsqrt(ms + NORM_EPS) * g


def _dot(a, b):
    return jnp.dot(a, b, preferred_element_type=F32)


def _dot_nt(a, b):
    return lax.dot_general(a, b, (((1,), (1,)), ((), ())), preferred_element_type=F32)


def _dot_tn(a, b):
    return lax.dot_general(a, b, (((0,), (0,)), ((), ())), preferred_element_type=F32)


def _split3(x):
    hi = x.astype(BF16)
    r1 = x - hi.astype(F32)
    mid = r1.astype(BF16)
    lo = (r1 - mid.astype(F32)).astype(BF16)
    return hi, mid, lo


def _norm_proj_kernel(x_ref, g_ref, w_ref, *out_refs, groups, nchunk):
    xn = _rms(x_ref[...], g_ref[...]).astype(BF16)
    for (c0, c1), o_ref in zip(groups, out_refs):
        for n0 in range(c0, c1, nchunk):
            n1 = min(n0 + nchunk, c1)
            o_ref[:, n0 - c0:n1 - c0] = _dot(xn, w_ref[:, n0:n1]).astype(o_ref.dtype)


def _norm_proj(h, g, w, groups, dtypes):
    t = h.shape[0]
    n = w.shape[1]
    kern = functools.partial(_norm_proj_kernel, groups=groups, nchunk=512)
    return pl.pallas_call(
        kern,
        grid=(t // PROJ_TM,),
        in_specs=[
            pl.BlockSpec((PROJ_TM, D_MODEL), lambda i: (i, 0)),
            pl.BlockSpec((1, D_MODEL), lambda i: (0, 0)),
            pl.BlockSpec((D_MODEL, n), lambda i: (0, 0)),
        ],
        out_specs=[pl.BlockSpec((PROJ_TM, c1 - c0), lambda i: (i, 0)) for c0, c1 in groups],
        out_shape=[jax.ShapeDtypeStruct((t, c1 - c0), dt) for (c0, c1), dt in zip(groups, dtypes)],
        compiler_params=pltpu.CompilerParams(
            dimension_semantics=("arbitrary",), vmem_limit_bytes=VMEM_LIMIT),
    )(h, g.reshape(1, D_MODEL), w)


def _conv_kernel(ua_ref, w_ref, cb_ref, lg_ref, lb_ref, o_ref, hbuf, ysh):
    ts = CONV_TS
    t = pl.program_id(1)

    @pl.when(t == 0)
    def _():
        hbuf[0:CONV_HALO, :] = jnp.zeros((CONV_HALO, CONV_CH), F32)

    @pl.when(t > 0)
    def _():
        hbuf[0:CONV_HALO, :] = hbuf[ts:ts + CONV_HALO, :]

    a = ua_ref[:, 0:CONV_CH].astype(F32)
    gate = ua_ref[:, CONV_CH:2 * CONV_CH].astype(F32)
    hbuf[CONV_HALO:CONV_HALO + ts, :] = a * _sigmoid(gate)

    base = CONV_HALO - (CONV_WIDTH - 1)
    ylen = ysh.shape[1]
    for b in range(1, SUBLANES):
        ysh[b - 1] = hbuf[b:b + ylen, :]

    cb = cb_ref[...]
    lg = lg_ref[...]
    lb = lb_ref[...]
    for r in range(0, ts, CONV_RC):
        acc = jnp.zeros((CONV_RC, CONV_CH), F32)
        for s in range(base, base + CONV_WIDTH):
            a8, b = divmod(s, SUBLANES)
            r8 = r + a8 * SUBLANES
            tap = hbuf[r8:r8 + CONV_RC, :] if b == 0 else ysh[b - 1, r8:r8 + CONV_RC, :]
            acc = acc + tap * w_ref[s - base:s - base + 1, :]
        acc = acc + cb
        mu = jnp.mean(acc, axis=-1, keepdims=True)
        d = acc - mu
        var = jnp.mean(d * d, axis=-1, keepdims=True)
        y = d * lax.rsqrt(var + NORM_EPS) * lg + lb
        o_ref[r:r + CONV_RC, :] = (y * _sigmoid(y)).astype(o_ref.dtype)


def _conv_module(main, conv_w, conv_b, ln_g, ln_b, batch, seq):
    t = main.shape[0]
    nt = seq // CONV_TS
    w = jnp.concatenate([conv_w.reshape(CONV_WIDTH, CONV_CH),
                         jnp.zeros((1, CONV_CH), F32)], axis=0)
    vec = lambda v: v.reshape(1, CONV_CH)
    const = lambda b, i: (0, 0)
    return pl.pallas_call(
        _conv_kernel,
        grid=(batch, nt),
        in_specs=[
            pl.BlockSpec((CONV_TS, 2 * CONV_CH), lambda b, i: (b * nt + i, 0)),
            pl.BlockSpec((CONV_WIDTH + 1, CONV_CH), const),
            pl.BlockSpec((1, CONV_CH), const),
            pl.BlockSpec((1, CONV_CH), const),
            pl.BlockSpec((1, CONV_CH), const),
        ],
        out_specs=pl.BlockSpec((CONV_TS, CONV_CH), lambda b, i: (b * nt + i, 0)),
        out_shape=jax.ShapeDtypeStruct((t, CONV_CH), BF16),
        scratch_shapes=[pltpu.VMEM((CONV_HALO + CONV_TS, CONV_CH), F32),
                        pltpu.VMEM((SUBLANES - 1, CONV_TS + CONV_HALO - SUBLANES, CONV_CH), F32)],
        compiler_params=pltpu.CompilerParams(
            dimension_semantics=("arbitrary", "arbitrary"), vmem_limit_bytes=VMEM_LIMIT),
    )(main, w, vec(conv_b), vec(ln_g), vec(ln_b))


def _rope_group(xg, c, a, bm):
    return xg * c + pltpu.roll(xg, LANES - ROPE_ROT // 2, 1) * a + pltpu.roll(xg, ROPE_ROT // 2, 1) * bm


def _count_ge(x, c):
    return jnp.sum(jnp.where(x >= c, 1.0, 0.0), axis=0, keepdims=True)


def _count_ge_packed(db_ref, kv, cand):
    tq = db_ref.shape[1]
    cand_b = jnp.broadcast_to(cand, (BF16_ROWS, tq)).astype(BF16)
    one = jnp.ones((BF16_ROWS, tq), BF16)
    zero = jnp.zeros((BF16_ROWS, tq), BF16)
    parts = [jnp.where(db_ref[r:r + BF16_ROWS, :] >= cand_b, one, zero)
             for r in range(0, kv, BF16_ROWS)]
    while len(parts) > 1:
        nxt = [a + b for a, b in zip(parts[0::2], parts[1::2])]
        if len(parts) % 2:
            nxt.append(parts[-1])
        parts = nxt
    return jnp.sum(parts[0].astype(F32), axis=0, keepdims=True)


def _select_topk(sc_s, db_s, bt_s, kv, kf, causal, few):
    tq = sc_s.shape[1]
    neg_inf = -jnp.inf
    s_all = sc_s[0:kv, :]
    mx = jnp.max(s_all, axis=0, keepdims=True)
    mn = jnp.min(jnp.where(causal, s_all, jnp.inf), axis=0, keepdims=True)
    hi_top = mx + jnp.maximum((mx - mn) * (1.0 / 64.0), jnp.abs(mx) * (2.0 ** -20) + 1e-30)

    lo, hi = mn, hi_top
    for _ in range(TOPK_LEVELS):
        step = jnp.maximum(hi - lo, 1e-30) * (1.0 / 256.0)
        inv = 1.0 / step
        db_s[0:kv, :] = jnp.clip((sc_s[0:kv, :] - lo) * inv, -1.0, 257.0).astype(BF16)

        def digit(_, carry):
            c, bit = carry
            cand = c + bit
            cnt = _count_ge_packed(db_s, kv, cand)
            return jnp.where(cnt >= kf, cand, c), bit * 0.5

        c, _ = lax.fori_loop(0, 8, digit, (jnp.zeros((1, tq), F32), jnp.full((1, tq), 128.0, F32)))
        lo, hi = lo + (c - 1.0) * step, lo + (c + 2.0) * step

    def verify(hi):
        s = sc_s[0:kv, :]
        below = s < hi
        vstar = jnp.max(jnp.where(below, s, neg_inf), axis=0, keepdims=True)
        c_hi = float(kv) - jnp.sum(jnp.where(below, 1.0, 0.0), axis=0, keepdims=True)
        return vstar, c_hi, _count_ge(s, vstar)

    def is_ok(c_hi, c_ge):
        ok = jnp.logical_and(c_hi < kf, c_ge >= kf)
        return ok if few is None else jnp.logical_or(ok, few)

    def unfinished(st):
        _, _, _, c_hi, c_ge, n = st
        bad = jnp.max(jnp.where(is_ok(c_hi, c_ge), 0.0, 1.0))
        return jnp.logical_and(bad > 0.5, n < TOPK_MAX_FIX_ROUNDS)

    def fix(st):
        lo, hi, vstar, c_hi, c_ge, n = st
        bad = jnp.logical_not(is_ok(c_hi, c_ge))
        hi_true = jnp.logical_and(bad, c_hi >= kf)
        lo = jnp.where(hi_true, hi, lo)
        hi = jnp.where(hi_true, hi_top, jnp.where(jnp.logical_and(bad, c_ge < kf), vstar, hi))
        mid = 0.5 * lo + 0.5 * hi
        up = _count_ge(sc_s[0:kv, :], mid) >= kf
        lo = jnp.where(jnp.logical_and(bad, up), mid, lo)
        hi = jnp.where(jnp.logical_and(bad, jnp.logical_not(up)), mid, hi)
        vstar, c_hi, c_ge = verify(hi)
        return lo, hi, vstar, c_hi, c_ge, n + 1

    vstar, c_hi, c_ge = verify(hi)
    _, _, vstar, c_hi, c_ge, _ = lax.while_loop(
        unfinished, fix, (mn, hi, vstar, c_hi, c_ge, jnp.int32(0)))

    need = kf - c_hi
    if few is not None:
        vstar = jnp.where(few, neg_inf, vstar)
        c_ge = jnp.where(few, 0.0, c_ge)
    has_extra_ties = jnp.max(c_ge) > kf

    @pl.when(jnp.logical_not(has_extra_ties))
    def _():
        keep = sc_s[0:kv, :] >= vstar
        if few is not None:
            keep = jnp.logical_and(keep, causal)
        bt_s[0:kv, :] = jnp.where(keep, 0.0, neg_inf)

    @pl.when(has_extra_ties)
    def _():
        ri = lax.broadcasted_iota(jnp.int32, (TIE_CHUNK, TIE_CHUNK), 0)
        ci = lax.broadcasted_iota(jnp.int32, (TIE_CHUNK, TIE_CHUNK), 1)
        tri = jnp.where(ci <= ri, 1.0, 0.0).astype(BF16)
        seen = jnp.zeros((1, tq), F32)
        for c0 in range(0, kv, TIE_CHUNK):
            s = sc_s[c0:c0 + TIE_CHUNK, :]
            eq = jnp.where(s == vstar, 1.0, 0.0)
            rank = _dot(tri, eq.astype(BF16)) + seen
            seen = seen + jnp.sum(eq, axis=0, keepdims=True)
            keep = jnp.logical_or(s > vstar, jnp.logical_and(s == vstar, rank <= need))
            if few is not None:
                keep = jnp.logical_and(keep, causal[c0:c0 + TIE_CHUNK, :])
            bt_s[c0:c0 + TIE_CHUNK, :] = jnp.where(keep, 0.0, neg_inf)


def _dsa_kernel(q_ref, iq_ref, small_ref, tc_ref, ta_ref, tb_ref, o_ref,
                k_s, ik_s, v_s, sc_s, db_s, bt_s, bias_s, *, seq):
    tq = DSA_TQ
    w_scale = (IDX_HEADS ** -0.5) * (HEAD_DIM ** -0.5)
    att_scale = HEAD_DIM ** -0.5
    n_sel = min(IDX_TOPK, seq // 4)
    lane = lax.broadcasted_iota(jnp.int32, (1, LANES), 1)
    lo_half = lane < HEAD_DIM

    tc = tc_ref[...]
    ta = ta_ref[...]
    tb = tb_ref[...]
    k_s[...] = _rope_group(small_ref[:, 0:LANES], tc, ta, tb).astype(BF16)
    ik_s[...] = _rope_group(small_ref[:, LANES:2 * LANES], tc, ta, tb).astype(BF16)
    v_s[...] = small_ref[:, 2 * LANES:3 * LANES].astype(BF16)

    def rope_split(ref, r0, g, scale=None):
        xg = ref[r0:r0 + tq, g * LANES:(g + 1) * LANES].astype(F32)
        xr = _rope_group(xg, tc_ref[r0:r0 + tq, :], ta_ref[r0:r0 + tq, :], tb_ref[r0:r0 + tq, :])
        if scale is not None:
            xr = xr * scale
        return (jnp.where(lo_half, xr, 0.0).astype(BF16), jnp.where(lo_half, 0.0, xr).astype(BF16))

    for blk in range(seq // tq):
        r0 = blk * tq
        kv = r0 + tq
        iw_t = jnp.transpose(small_ref[r0:r0 + tq, 3 * LANES:4 * LANES])
        ik = ik_s[0:kv, :]
        score = jnp.zeros((kv, tq), F32)
        for g in range(IDX_HEADS // 2):
            halves = rope_split(iq_ref, r0, g)
            for par in range(2):
                hh = 2 * g + par
                logits = _dot_nt(ik, halves[par])
                score = score + jnp.maximum(logits, 0.0) * iw_t[hh:hh + 1, :]
        score = score * w_scale
        key_pos = lax.broadcasted_iota(jnp.int32, (kv, tq), 0)
        q_pos = r0 + lax.broadcasted_iota(jnp.int32, (kv, tq), 1)
        causal = key_pos <= q_pos
        sc_s[0:kv, :] = jnp.where(causal, score, -jnp.inf)

        if kv <= n_sel:
            bt_s[0:kv, :] = jnp.where(
                jnp.logical_and(causal, sc_s[0:kv, :] >= -jnp.inf), 0.0, -jnp.inf)
        else:
            few = None
            if r0 < n_sel:
                few = r0 + lax.broadcasted_iota(jnp.int32, (1, tq), 1) < n_sel
            _select_topk(sc_s, db_s, bt_s, kv, float(n_sel), causal, few)
        bias_s[:, 0:kv] = jnp.transpose(bt_s[0:kv, :])

        kk = k_s[0:kv, :]
        vv = v_s[0:kv, :]
        for g in range(ATT_HEADS // 2):
            halves = rope_split(q_ref, r0, g, att_scale)
            outs = []
            for par in range(2):
                s = _dot_nt(halves[par], kk) + bias_s[:, 0:kv]
                m = jnp.max(s, axis=1, keepdims=True)
                p = jnp.exp(s - m)
                l = jnp.sum(p, axis=1, keepdims=True)
                outs.append(_dot(p.astype(BF16), vv) / l)
            o_ref[r0:r0 + tq, g * LANES:(g + 1) * LANES] = (
                jnp.where(lo_half, outs[0], outs[1]).astype(o_ref.dtype))


def _dsa(main, small, tabs, batch, seq):
    t = main.shape[0]
    nh = ATT_HEADS * HEAD_DIM
    const = lambda b: (0, 0)
    kern = functools.partial(_dsa_kernel, seq=seq)
    return pl.pallas_call(
        kern,
        grid=(batch,),
        in_specs=[
            pl.BlockSpec((seq, nh), lambda b: (b, 2)),
            pl.BlockSpec((seq, nh), lambda b: (b, 3)),
            pl.BlockSpec((seq, 4 * LANES), lambda b: (b, 0)),
            pl.BlockSpec((seq, LANES), const),
            pl.BlockSpec((seq, LANES), const),
            pl.BlockSpec((seq, LANES), const),
        ],
        out_specs=pl.BlockSpec((seq, nh), lambda b: (b, 0)),
        out_shape=jax.ShapeDtypeStruct((t, nh), BF16),
        scratch_shapes=[
            pltpu.VMEM((seq, LANES), BF16),
            pltpu.VMEM((seq, LANES), BF16),
            pltpu.VMEM((seq, LANES), BF16),
            pltpu.VMEM((seq, DSA_TQ), F32),
            pltpu.VMEM((seq, DSA_TQ), BF16),
            pltpu.VMEM((seq, DSA_TQ), F32),
            pltpu.VMEM((DSA_TQ, seq), F32),
        ],
        compiler_params=pltpu.CompilerParams(
            dimension_semantics=("arbitrary",), vmem_limit_bytes=VMEM_LIMIT),
    )(main, main, small, *tabs)


def _rope_lane_tables(seq):
    half = ROPE_ROT // 2
    inv = ROPE_THETA ** (-jnp.arange(0, ROPE_ROT, 2, dtype=F32) / ROPE_ROT)
    ang = jnp.arange(seq, dtype=F32)[:, None] * inv[None, :]
    cos, sin = jnp.cos(ang), jnp.sin(ang)
    rest = HEAD_DIM - ROPE_ROT
    c64 = jnp.concatenate([cos, cos, jnp.ones((seq, rest), F32)], axis=1)
    a64 = jnp.concatenate([-sin, jnp.zeros((seq, HEAD_DIM - half), F32)], axis=1)
    b64 = jnp.concatenate([jnp.zeros((seq, half), F32), sin, jnp.zeros((seq, rest), F32)], axis=1)
    return tuple(jnp.tile(x, (1, LANES // HEAD_DIM)) for x in (c64, a64, b64))


def _gla_kernel(q_ref, k_ref, v_ref, r_ref, glr_ref, gw_ref, gb_ref, og_ref, o_ref,
                state, bc_s, qt_s, kt_s, ksc_s):
    tb, c, sb = GLA_TB, GLA_CHUNK, GLA_SUB

    @pl.when(pl.program_id(1) == 0)
    def _():
        state[...] = jnp.zeros(state.shape, F32)

    ri = lax.broadcasted_iota(jnp.int32, (sb, sb), 0)
    ci = lax.broadcasted_iota(jnp.int32, (sb, sb), 1)
    shift = c.bit_length() - 1
    tri = jnp.logical_and(ci <= ri, lax.shift_right_logical(ri, shift) == lax.shift_right_logical(ci, shift))
    tri_b = jnp.where(tri, 1.0, 0.0).astype(BF16)
    gw_h, gw_m, gw_l = _split3(gw_ref[...])
    gb = gb_ref[...]

    for s0 in range(0, tb, sb):
        rows = slice(s0, s0 + sb)
        gl_h, gl_m, gl_l = _split3(glr_ref[rows, :])
        pre = (_dot(gl_h, gw_h) + (_dot(gl_h, gw_m) + _dot(gl_m, gw_h))
               + (_dot(gl_h, gw_l) + _dot(gl_m, gw_m) + _dot(gl_l, gw_h))) + gb
        log_a = (jnp.minimum(pre, 0.0) - jnp.log(1.0 + jnp.exp(-jnp.abs(pre)))) / GLA_TAU
        la_h, la_m, la_l = _split3(log_a)
        bcum = _dot(tri_b, la_h) + _dot(tri_b, la_m) + _dot(tri_b, la_l)
        bc_s[rows, :] = bcum
        qt_s[rows, :] = (q_ref[rows, :].astype(F32) * (GLA_DK ** -0.5) * jnp.exp(bcum)).astype(BF16)
        kt_s[rows, :] = (k_ref[rows, :].astype(F32) * jnp.exp(-bcum)).astype(BF16)
    for t0 in range(0, tb, c):
        rows = slice(t0, t0 + c)
        b_last = bc_s[t0 + c - 1:t0 + c, :]
        ksc_s[rows, :] = (k_ref[rows, :].astype(F32) * jnp.exp(b_last - bc_s[rows, :])).astype(BF16)

    og = og_ref[...]
    cmask = (lax.broadcasted_iota(jnp.int32, (c, c), 1) <= lax.broadcasted_iota(jnp.int32, (c, c), 0))
    for hd in range(GLA_HEADS):
        ks = slice(hd * GLA_DK, (hd + 1) * GLA_DK)
        vs = slice(hd * GLA_DV, (hd + 1) * GLA_DV)
        st = state[hd]
        for t0 in range(0, tb, c):
            rows = slice(t0, t0 + c)
            qh = qt_s[rows, ks]
            vh = v_ref[rows, vs]
            att = jnp.where(cmask, _dot_nt(qh, kt_s[rows, ks]), 0.0)
            o = _dot(att.astype(BF16), vh) + _dot_nt(qh, st.astype(BF16))
            st = st * jnp.exp(bc_s[t0 + c - 1:t0 + c, ks]) + _dot_tn(vh, ksc_s[rows, ks])
            o = o * lax.rsqrt(jnp.mean(o * o, axis=-1, keepdims=True) + NORM_EPS) * og[:, vs]
            rr = r_ref[rows, vs].astype(F32)
            o_ref[rows, vs] = (o * (rr * _sigmoid(rr))).astype(o_ref.dtype)
        state[hd] = st


def _gla(main, glr, gate_w, gate_b, onorm_g, batch, seq):
    t = main.shape[0]
    nt = seq // GLA_TB
    nk = GLA_HEADS * GLA_DK
    nv = GLA_HEADS * GLA_DV
    gw = jnp.concatenate([gate_w, jnp.zeros((LANES - GLA_GATE_RANK, nk), F32)], axis=0)
    const = lambda b, i: (0, 0)
    row = lambda b, i: b * nt + i
    return pl.pallas_call(
        _gla_kernel,
        grid=(batch, nt),
        in_specs=[
            pl.BlockSpec((GLA_TB, nk), lambda b, i: (row(b, i), 0)),
            pl.BlockSpec((GLA_TB, nk), lambda b, i: (row(b, i), 1)),
            pl.BlockSpec((GLA_TB, nv), lambda b, i: (row(b, i), 1)),
            pl.BlockSpec((GLA_TB, nv), lambda b, i: (row(b, i), 2)),
            pl.BlockSpec((GLA_TB, LANES), lambda b, i: (row(b, i), 0)),
            pl.BlockSpec((LANES, nk), const),
            pl.BlockSpec((1, nk), const),
            pl.BlockSpec((1, nv), const),
        ],
        out_specs=pl.BlockSpec((GLA_TB, nv), lambda b, i: (row(b, i), 0)),
        out_shape=jax.ShapeDtypeStruct((t, nv), BF16),
        scratch_shapes=[pltpu.VMEM((GLA_HEADS, GLA_DV, GLA_DK), F32),
                        pltpu.VMEM((GLA_TB, nk), F32),
                        pltpu.VMEM((GLA_TB, nk), BF16),
                        pltpu.VMEM((GLA_TB, nk), BF16),
                        pltpu.VMEM((GLA_TB, nk), BF16)],
        compiler_params=pltpu.CompilerParams(
            dimension_semantics=("arbitrary", "arbitrary"), vmem_limit_bytes=VMEM_LIMIT),
    )(main, main, main, main, glr, gw, gate_b.reshape(1, nk), onorm_g.reshape(1, nv))


def _out_ffn_kernel(*refs, y_widths, final):
    n_y = len(y_widths)
    h_ref = refs[0]
    y_refs = refs[1:1 + n_y]
    wo_ref, g_ref, wg_ref, wu_ref, wd_ref, fg_ref, o_ref, a_scr = refs[1 + n_y:]

    acc = h_ref[...]
    k0 = 0
    for y_ref, kw in zip(y_refs, y_widths):
        acc = acc + _dot(y_ref[...], wo_ref[k0:k0 + kw, :])
        k0 += kw
    o_ref[...] = acc
    hn = _rms(acc, g_ref[...]).astype(BF16)
    for f0 in range(0, D_FF, FFN_FC):
        gt = _dot(hn, wg_ref[:, f0:f0 + FFN_FC])
        up = _dot(hn, wu_ref[:, f0:f0 + FFN_FC])
        a_scr[:, f0:f0 + FFN_FC] = (gt * _sigmoid(gt) * up).astype(BF16)
    out = o_ref[...] + _dot(a_scr[...], wd_ref[...])
    if final:
        out = _rms(out, fg_ref[...])
    o_ref[...] = out


def _out_ffn(h, ys, w_out, g, w_gate, w_up, w_down, final_g, final):
    t = h.shape[0]
    y_widths = tuple(y.shape[1] for y in ys)
    const = lambda i: (0, 0)
    kern = functools.partial(_out_ffn_kernel, y_widths=y_widths, final=final)
    resident = lambda shape: pl.BlockSpec(shape, const, pipeline_mode=pl.Buffered(1))
    return pl.pallas_call(
        kern,
        grid=(t // FFN_TM,),
        in_specs=(
            [pl.BlockSpec((FFN_TM, D_MODEL), lambda i: (i, 0))]
            + [pl.BlockSpec((FFN_TM, kw), lambda i: (i, 0)) for kw in y_widths]
            + [resident((sum(y_widths), D_MODEL)),
               pl.BlockSpec((1, D_MODEL), const),
               resident((D_MODEL, D_FF)),
               resident((D_MODEL, D_FF)),
               resident((D_FF, D_MODEL)),
               pl.BlockSpec((1, D_MODEL), const)]),
        out_specs=pl.BlockSpec((FFN_TM, D_MODEL), lambda i: (i, 0)),
        out_shape=jax.ShapeDtypeStruct((t, D_MODEL), F32),
        scratch_shapes=[pltpu.VMEM((FFN_TM, D_FF), BF16)],
        compiler_params=pltpu.CompilerParams(
            dimension_semantics=("arbitrary",), vmem_limit_bytes=VMEM_LIMIT),
    )(h, *ys, w_out.astype(BF16), g.reshape(1, D_MODEL), w_gate.astype(BF16),
      w_up.astype(BF16), w_down.astype(BF16), final_g.reshape(1, D_MODEL))


def kernel(x, norm_mix_g, ab_w_in, ab_conv_w, ab_conv_b, ab_ln_g, ab_ln_b, ab_w_out,
           c_w_in, c_gate_w, c_gate_b, c_onorm_g, c_w_out,
           norm_ffn_g, ffn_w_gate, ffn_w_up, ffn_w_down, final_norm_g):
    batch, seq, _ = x.shape
    t = batch * seq
    h = x.reshape(t, D_MODEL)
    nh = ATT_HEADS * HEAD_DIM

    w = ab_w_in[0]
    o_q = 2 * CONV_CH
    o_k = o_q + nh
    o_v = o_k + HEAD_DIM
    o_iq = o_v + HEAD_DIM
    o_ik = o_iq + IDX_HEADS * HEAD_DIM
    o_iw = o_ik + HEAD_DIM
    w_k, w_v, w_ik = w[:, o_k:o_v], w[:, o_v:o_iq], w[:, o_ik:o_iw]
    w_iw = jnp.concatenate([w[:, o_iw:], jnp.zeros((D_MODEL, LANES - IDX_HEADS), F32)], axis=1)
    w_all = jnp.concatenate(
        [w[:, :o_k], w[:, o_iq:o_ik], w_k, w_k, w_ik, w_ik, w_v, w_v, w_iw], axis=1).astype(BF16)
    n_main = 2 * CONV_CH + 2 * nh
    main, small = _norm_proj(h, norm_mix_g[0], w_all,
                             ((0, n_main), (n_main, n_main + 4 * LANES)), (BF16, F32))
    ya = _conv_module(main, ab_conv_w[0], ab_conv_b[0], ab_ln_g[0], ab_ln_b[0], batch, seq)
    yb = _dsa(main, small, _rope_lane_tables(seq), batch, seq)
    h = _out_ffn(h, (ya, yb), ab_w_out[0], norm_ffn_g[0], ffn_w_gate[0], ffn_w_up[0],
                 ffn_w_down[0], final_norm_g, False)

    w = c_w_in[0]
    n_main = 2 * GLA_HEADS * GLA_DK + 2 * GLA_HEADS * GLA_DV
    w_all = jnp.concatenate(
        [w, jnp.zeros((D_MODEL, LANES - GLA_GATE_RANK), F32)], axis=1).astype(BF16)
    main, glr = _norm_proj(h, norm_mix_g[1], w_all,
                           ((0, n_main), (n_main, n_main + LANES)), (BF16, F32))
    yc = _gla(main, glr, c_gate_w[0], c_gate_b[0], c_onorm_g[0], batch, seq)
    h = _out_ffn(h, (yc,), c_w_out[0], norm_ffn_g[1], ffn_w_gate[1], ffn_w_up[1],
                 ffn_w_down[1], final_norm_g, True)
    return h.reshape(batch, seq, D_MODEL)
```

```python
import functools

import jax
import jax.numpy as jnp
from jax import lax
from jax.experimental import pallas as pl
from jax.experimental.pallas import tpu as pltpu

F32 = jnp.float32
BF16 = jnp.bfloat16

D_MODEL = 1024
CONV_CH = 512
CONV_WIDTH = 31
ATT_HEADS = 8
HEAD_DIM = 64
IDX_HEADS = 8
IDX_TOPK = 256
GLA_HEADS = 4
GLA_DK = 128
GLA_DV = 256
GLA_GATE_RANK = 16
GLA_TAU = 16.0
GLA_CHUNK = 64
ROPE_THETA = 500000.0
ROPE_ROT = HEAD_DIM // 4
D_FF = 2816
NORM_EPS = 1e-6

LANES = 128
SUBLANES = 8
VMEM_LIMIT = 56 * 1024 * 1024

PROJ_TM = 512
FFN_TM = 512
FFN_FC = 256
CONV_TS = 256
CONV_RC = 32
CONV_HALO = 32
DSA_TQ = 256
GLA_TB = 512
GLA_SUB = 256
BF16_ROWS = 16
F32_EPS = 2.0 ** -23
TOPK_LEVELS = 4
TOPK_MAX_FIX_ROUNDS = 400
TIE_CHUNK = 256


def _sigmoid(x):
    return 1.0 / (1.0 + jnp.exp(-x))


def _rms(x, g):
    ms = jnp.mean(x * x, axis=-1, keepdims=True)
    return x * lax.rsqrt(ms + NORM_EPS) * g


def _dot(a, b):
    return jnp.dot(a, b, preferred_element_type=F32)


def _dot_nt(a, b):
    return lax.dot_general(a, b, (((1,), (1,)), ((), ())), preferred_element_type=F32)


def _dot_tn(a, b):
    return lax.dot_general(a, b, (((0,), (0,)), ((), ())), preferred_element_type=F32)


def _split3(x):
    hi = x.astype(BF16)
    r1 = x - hi.astype(F32)
    mid = r1.astype(BF16)
    lo = (r1 - mid.astype(F32)).astype(BF16)
    return hi, mid, lo


def _norm_proj_kernel(x_ref, g_ref, w_ref, *out_refs, groups, nchunk):
    xn = _rms(x_ref[...], g_ref[...]).astype(BF16)
    for (c0, c1), o_ref in zip(groups, out_refs):
        for n0 in range(c0, c1, nchunk):
            n1 = min(n0 + nchunk, c1)
            o_ref[:, n0 - c0:n1 - c0] = _dot(xn, w_ref[:, n0:n1]).astype(o_ref.dtype)


def _norm_proj(h, g, w, groups, dtypes):
    t = h.shape[0]
    n = w.shape[1]
    kern = functools.partial(_norm_proj_kernel, groups=groups, nchunk=512)
    return pl.pallas_call(
        kern,
        grid=(t // PROJ_TM,),
        in_specs=[
            pl.BlockSpec((PROJ_TM, D_MODEL), lambda i: (i, 0)),
            pl.BlockSpec((1, D_MODEL), lambda i: (0, 0)),
            pl.BlockSpec((D_MODEL, n), lambda i: (0, 0)),
        ],
        out_specs=[pl.BlockSpec((PROJ_TM, c1 - c0), lambda i: (i, 0)) for c0, c1 in groups],
        out_shape=[jax.ShapeDtypeStruct((t, c1 - c0), dt) for (c0, c1), dt in zip(groups, dtypes)],
        compiler_params=pltpu.CompilerParams(
            dimension_semantics=("arbitrary",), vmem_limit_bytes=VMEM_LIMIT),
    )(h, g.reshape(1, D_MODEL), w)


def _conv_kernel(ua_ref, w_ref, cb_ref, lg_ref, lb_ref, o_ref, hbuf, ysh):
    ts = CONV_TS
    t = pl.program_id(1)

    @pl.when(t == 0)
    def _():
        hbuf[0:CONV_HALO, :] = jnp.zeros((CONV_HALO, CONV_CH), F32)

    @pl.when(t > 0)
    def _():
        hbuf[0:CONV_HALO, :] = hbuf[ts:ts + CONV_HALO, :]

    a = ua_ref[:, 0:CONV_CH].astype(F32)
    gate = ua_ref[:, CONV_CH:2 * CONV_CH].astype(F32)
    hbuf[CONV_HALO:CONV_HALO + ts, :] = a * _sigmoid(gate)

    base = CONV_HALO - (CONV_WIDTH - 1)
    ylen = ysh.shape[1]
    for b in range(1, SUBLANES):
        ysh[b - 1] = hbuf[b:b + ylen, :]

    cb = cb_ref[...]
    lg = lg_ref[...]
    lb = lb_ref[...]
    for r in range(0, ts, CONV_RC):
        acc = jnp.zeros((CONV_RC, CONV_CH), F32)
        for s in range(base, base + CONV_WIDTH):
            a8, b = divmod(s, SUBLANES)
            r8 = r + a8 * SUBLANES
            tap = hbuf[r8:r8 + CONV_RC, :] if b == 0 else ysh[b - 1, r8:r8 + CONV_RC, :]
            acc = acc + tap * w_ref[s - base:s - base + 1, :]
        acc = acc + cb
        mu = jnp.mean(acc, axis=-1, keepdims=True)
        d = acc - mu
        var = jnp.mean(d * d, axis=-1, keepdims=True)
        y = d * lax.rsqrt(var + NORM_EPS) * lg + lb
        o_ref[r:r + CONV_RC, :] = (y * _sigmoid(y)).astype(o_ref.dtype)


def _conv_module(main, conv_w, conv_b, ln_g, ln_b, batch, seq):
    t = main.shape[0]
    nt = seq // CONV_TS
    w = jnp.concatenate([conv_w.reshape(CONV_WIDTH, CONV_CH),
                         jnp.zeros((1, CONV_CH), F32)], axis=0)
    vec = lambda v: v.reshape(1, CONV_CH)
    const = lambda b, i: (0, 0)
    return pl.pallas_call(
        _conv_kernel,
        grid=(batch, nt),
        in_specs=[
            pl.BlockSpec((CONV_TS, 2 * CONV_CH), lambda b, i: (b * nt + i, 0)),
            pl.BlockSpec((CONV_WIDTH + 1, CONV_CH), const),
            pl.BlockSpec((1, CONV_CH), const),
            pl.BlockSpec((1, CONV_CH), const),
            pl.BlockSpec((1, CONV_CH), const),
        ],
        out_specs=pl.BlockSpec((CONV_TS, CONV_CH), lambda b, i: (b * nt + i, 0)),
        out_shape=jax.ShapeDtypeStruct((t, CONV_CH), BF16),
        scratch_shapes=[pltpu.VMEM((CONV_HALO + CONV_TS, CONV_CH), F32),
                        pltpu.VMEM((SUBLANES - 1, CONV_TS + CONV_HALO - SUBLANES, CONV_CH), F32)],
        compiler_params=pltpu.CompilerParams(
            dimension_semantics=("arbitrary", "arbitrary"), vmem_limit_bytes=VMEM_LIMIT),
    )(main, w, vec(conv_b), vec(ln_g), vec(ln_b))


def _rope_group(xg, c, a, bm):
    return xg * c + pltpu.roll(xg, LANES - ROPE_ROT // 2, 1) * a + pltpu.roll(xg, ROPE_ROT // 2, 1) * bm


def _count_ge(x, c):
    return jnp.sum(jnp.where(x >= c, 1.0, 0.0), axis=0, keepdims=True)


def _count_ge_packed(db_ref, kv, cand):
    tq = db_ref.shape[1]
    cand_b = jnp.broadcast_to(cand, (BF16_ROWS, tq)).astype(BF16)
    one = jnp.ones((BF16_ROWS, tq), BF16)
    zero = jnp.zeros((BF16_ROWS, tq), BF16)
    parts = [jnp.where(db_ref[r:r + BF16_ROWS, :] >= cand_b, one, zero)
             for r in range(0, kv, BF16_ROWS)]
    while len(parts) > 1:
        nxt = [a + b for a, b in zip(parts[0::2], parts[1::2])]
        if len(parts) % 2:
            nxt.append(parts[-1])
        parts = nxt
    return jnp.sum(parts[0].astype(F32), axis=0, keepdims=True)


def _select_topk(sc_s, db_s, bt_s, kv, kf, causal, few):
    tq = sc_s.shape[1]
    neg_inf = -jnp.inf
    s_all = sc_s[0:kv, :]
    mx = jnp.max(s_all, axis=0, keepdims=True)
    mn = jnp.min(jnp.where(causal, s_all, jnp.inf), axis=0, keepdims=True)
    hi_top = mx + jnp.maximum((mx - mn) * (1.0 / 64.0), jnp.abs(mx) * (2.0 ** -20) + 1e-30)

    lo, hi = mn, hi_top
    for _ in range(TOPK_LEVELS):
        ulp = jnp.maximum(jnp.maximum(jnp.abs(lo), jnp.abs(hi)) * F32_EPS, 1e-30)
        step = jnp.maximum((hi - lo) * (1.0 / 256.0), ulp)
        inv = 1.0 / step
        db_s[0:kv, :] = jnp.clip((sc_s[0:kv, :] - lo) * inv, -1.0, 257.0).astype(BF16)

        def digit(_, carry):
            c, bit = carry
            cand = c + bit
            cnt = _count_ge_packed(db_s, kv, cand)
            return jnp.where(cnt >= kf, cand, c), bit * 0.5

        c, _ = lax.fori_loop(0, 8, digit, (jnp.zeros((1, tq), F32), jnp.full((1, tq), 128.0, F32)))
        lo, hi = lo + (c - 1.0) * step, lo + (c + 2.0) * step

    def verify(hi):
        s = sc_s[0:kv, :]
        below = s < hi
        vstar = jnp.max(jnp.where(below, s, neg_inf), axis=0, keepdims=True)
        c_hi = float(kv) - jnp.sum(jnp.where(below, 1.0, 0.0), axis=0, keepdims=True)
        return vstar, c_hi, _count_ge(s, vstar)

    def is_ok(c_hi, c_ge):
        ok = jnp.logical_and(c_hi < kf, c_ge >= kf)
        return ok if few is None else jnp.logical_or(ok, few)

    def unfinished(st):
        _, _, _, _, c_hi, c_ge, n = st
        bad = jnp.max(jnp.where(is_ok(c_hi, c_ge), 0.0, 1.0))
        return jnp.logical_and(bad > 0.5, n < TOPK_MAX_FIX_ROUNDS)

    def fix(st):
        lo, hi, span, vstar, c_hi, c_ge, n = st
        bad = jnp.logical_not(is_ok(c_hi, c_ge))
        hi_true = jnp.logical_and(bad, c_hi >= kf)
        lo = jnp.where(hi_true, hi, lo)
        hi = jnp.where(hi_true, jnp.minimum(hi + span, hi_top),
                       jnp.where(jnp.logical_and(bad, c_ge < kf), vstar, hi))
        span = jnp.where(hi_true, span * 8.0, span)
        mid = 0.5 * lo + 0.5 * hi
        up = _count_ge(sc_s[0:kv, :], mid) >= kf
        lo = jnp.where(jnp.logical_and(bad, up), mid, lo)
        hi = jnp.where(jnp.logical_and(bad, jnp.logical_not(up)), mid, hi)
        vstar, c_hi, c_ge = verify(hi)
        return lo, hi, span, vstar, c_hi, c_ge, n + 1

    vstar, c_hi, c_ge = verify(hi)
    _, _, _, vstar, c_hi, c_ge, _ = lax.while_loop(
        unfinished, fix, (mn, hi, hi - lo, vstar, c_hi, c_ge, jnp.int32(0)))

    need = kf - c_hi
    if few is not None:
        vstar = jnp.where(few, neg_inf, vstar)
        c_ge = jnp.where(few, 0.0, c_ge)
    has_extra_ties = jnp.max(c_ge) > kf

    @pl.when(jnp.logical_not(has_extra_ties))
    def _():
        keep = sc_s[0:kv, :] >= vstar
        if few is not None:
            keep = jnp.logical_and(keep, causal)
        bt_s[0:kv, :] = jnp.where(keep, 0.0, neg_inf)

    @pl.when(has_extra_ties)
    def _():
        ri = lax.broadcasted_iota(jnp.int32, (TIE_CHUNK, TIE_CHUNK), 0)
        ci = lax.broadcasted_iota(jnp.int32, (TIE_CHUNK, TIE_CHUNK), 1)
        tri = jnp.where(ci <= ri, 1.0, 0.0).astype(BF16)
        seen = jnp.zeros((1, tq), F32)
        for c0 in range(0, kv, TIE_CHUNK):
            s = sc_s[c0:c0 + TIE_CHUNK, :]
            eq = jnp.where(s == vstar, 1.0, 0.0)
            rank = _dot(tri, eq.astype(BF16)) + seen
            seen = seen + jnp.sum(eq, axis=0, keepdims=True)
            keep = jnp.logical_or(s > vstar, jnp.logical_and(s == vstar, rank <= need))
            if few is not None:
                keep = jnp.logical_and(keep, causal[c0:c0 + TIE_CHUNK, :])
            bt_s[c0:c0 + TIE_CHUNK, :] = jnp.where(keep, 0.0, neg_inf)


def _dsa_kernel(q_ref, iq_ref, small_ref, tc_ref, ta_ref, tb_ref, o_ref,
                k_s, ik_s, v_s, sc_s, db_s, bt_s, bias_s, *, seq):
    tq = DSA_TQ
    w_scale = (IDX_HEADS ** -0.5) * (HEAD_DIM ** -0.5)
    att_scale = HEAD_DIM ** -0.5
    n_sel = min(IDX_TOPK, seq // 4)
    lane = lax.broadcasted_iota(jnp.int32, (1, LANES), 1)
    lo_half = lane < HEAD_DIM

    tc = tc_ref[...]
    ta = ta_ref[...]
    tb = tb_ref[...]
    k_s[...] = _rope_group(small_ref[:, 0:LANES], tc, ta, tb).astype(BF16)
    ik_s[...] = _rope_group(small_ref[:, LANES:2 * LANES], tc, ta, tb).astype(BF16)
    v_s[...] = small_ref[:, 2 * LANES:3 * LANES].astype(BF16)

    def rope_split(ref, r0, g, scale=None):
        xg = ref[r0:r0 + tq, g * LANES:(g + 1) * LANES].astype(F32)
        xr = _rope_group(xg, tc_ref[r0:r0 + tq, :], ta_ref[r0:r0 + tq, :], tb_ref[r0:r0 + tq, :])
        if scale is not None:
            xr = xr * scale
        return (jnp.where(lo_half, xr, 0.0).astype(BF16), jnp.where(lo_half, 0.0, xr).astype(BF16))

    for blk in range(seq // tq):
        r0 = blk * tq
        kv = r0 + tq
        iw_t = jnp.transpose(small_ref[r0:r0 + tq, 3 * LANES:4 * LANES])
        ik = ik_s[0:kv, :]
        score = jnp.zeros((kv, tq), F32)
        for g in range(IDX_HEADS // 2):
            halves = rope_split(iq_ref, r0, g)
            for par in range(2):
                hh = 2 * g + par
                logits = _dot_nt(ik, halves[par])
                score = score + jnp.maximum(logits, 0.0) * iw_t[hh:hh + 1, :]
        score = score * w_scale
        key_pos = lax.broadcasted_iota(jnp.int32, (kv, tq), 0)
        q_pos = r0 + lax.broadcasted_iota(jnp.int32, (kv, tq), 1)
        causal = key_pos <= q_pos
        sc_s[0:kv, :] = jnp.where(causal, score, -jnp.inf)

        if kv <= n_sel:
            bt_s[0:kv, :] = jnp.where(
                jnp.logical_and(causal, sc_s[0:kv, :] >= -jnp.inf), 0.0, -jnp.inf)
        else:
            few = None
            if r0 < n_sel:
                few = r0 + lax.broadcasted_iota(jnp.int32, (1, tq), 1) < n_sel
            _select_topk(sc_s, db_s, bt_s, kv, float(n_sel), causal, few)
        bias_s[:, 0:kv] = jnp.transpose(bt_s[0:kv, :])

        kk = k_s[0:kv, :]
        vv = v_s[0:kv, :]
        for g in range(ATT_HEADS // 2):
            halves = rope_split(q_ref, r0, g, att_scale)
            outs = []
            for par in range(2):
                s = _dot_nt(halves[par], kk) + bias_s[:, 0:kv]
                m = jnp.max(s, axis=1, keepdims=True)
                p = jnp.exp(s - m)
                l = jnp.sum(p, axis=1, keepdims=True)
                outs.append(_dot(p.astype(BF16), vv) / l)
            o_ref[r0:r0 + tq, g * LANES:(g + 1) * LANES] = (
                jnp.where(lo_half, outs[0], outs[1]).astype(o_ref.dtype))


def _dsa(main, small, tabs, batch, seq):
    t = main.shape[0]
    nh = ATT_HEADS * HEAD_DIM
    const = lambda b: (0, 0)
    kern = functools.partial(_dsa_kernel, seq=seq)
    return pl.pallas_call(
        kern,
        grid=(batch,),
        in_specs=[
            pl.BlockSpec((seq, nh), lambda b: (b, 2)),
            pl.BlockSpec((seq, nh), lambda b: (b, 3)),
            pl.BlockSpec((seq, 4 * LANES), lambda b: (b, 0)),
            pl.BlockSpec((seq, LANES), const),
            pl.BlockSpec((seq, LANES), const),
            pl.BlockSpec((seq, LANES), const),
        ],
        out_specs=pl.BlockSpec((seq, nh), lambda b: (b, 0)),
        out_shape=jax.ShapeDtypeStruct((t, nh), BF16),
        scratch_shapes=[
            pltpu.VMEM((seq, LANES), BF16),
            pltpu.VMEM((seq, LANES), BF16),
            pltpu.VMEM((seq, LANES), BF16),
            pltpu.VMEM((seq, DSA_TQ), F32),
            pltpu.VMEM((seq, DSA_TQ), BF16),
            pltpu.VMEM((seq, DSA_TQ), F32),
            pltpu.VMEM((DSA_TQ, seq), F32),
        ],
        compiler_params=pltpu.CompilerParams(
            dimension_semantics=("arbitrary",), vmem_limit_bytes=VMEM_LIMIT),
    )(main, main, small, *tabs)


def _rope_lane_tables(seq):
    half = ROPE_ROT // 2
    inv = ROPE_THETA ** (-jnp.arange(0, ROPE_ROT, 2, dtype=F32) / ROPE_ROT)
    ang = jnp.arange(seq, dtype=F32)[:, None] * inv[None, :]
    cos, sin = jnp.cos(ang), jnp.sin(ang)
    rest = HEAD_DIM - ROPE_ROT
    c64 = jnp.concatenate([cos, cos, jnp.ones((seq, rest), F32)], axis=1)
    a64 = jnp.concatenate([-sin, jnp.zeros((seq, HEAD_DIM - half), F32)], axis=1)
    b64 = jnp.concatenate([jnp.zeros((seq, half), F32), sin, jnp.zeros((seq, rest), F32)], axis=1)
    return tuple(jnp.tile(x, (1, LANES // HEAD_DIM)) for x in (c64, a64, b64))


def _gla_kernel(q_ref, k_ref, v_ref, r_ref, glr_ref, gw_ref, gb_ref, og_ref, o_ref,
                state, bc_s, qt_s, kt_s, ksc_s):
    tb, c, sb = GLA_TB, GLA_CHUNK, GLA_SUB

    @pl.when(pl.program_id(1) == 0)
    def _():
        state[...] = jnp.zeros(state.shape, F32)

    ri = lax.broadcasted_iota(jnp.int32, (sb, sb), 0)
    ci = lax.broadcasted_iota(jnp.int32, (sb, sb), 1)
    shift = c.bit_length() - 1
    tri = jnp.logical_and(ci <= ri, lax.shift_right_logical(ri, shift) == lax.shift_right_logical(ci, shift))
    tri_b = jnp.where(tri, 1.0, 0.0).astype(BF16)
    gw_h, gw_m, gw_l = _split3(gw_ref[...])
    gb = gb_ref[...]

    for s0 in range(0, tb, sb):
        rows = slice(s0, s0 + sb)
        gl_h, gl_m, gl_l = _split3(glr_ref[rows, :])
        pre = (_dot(gl_h, gw_h) + (_dot(gl_h, gw_m) + _dot(gl_m, gw_h))
               + (_dot(gl_h, gw_l) + _dot(gl_m, gw_m) + _dot(gl_l, gw_h))) + gb
        log_a = (jnp.minimum(pre, 0.0) - jnp.log(1.0 + jnp.exp(-jnp.abs(pre)))) / GLA_TAU
        la_h, la_m, la_l = _split3(log_a)
        bcum = _dot(tri_b, la_h) + _dot(tri_b, la_m) + _dot(tri_b, la_l)
        bc_s[rows, :] = bcum
        qt_s[rows, :] = (q_ref[rows, :].astype(F32) * (GLA_DK ** -0.5) * jnp.exp(bcum)).astype(BF16)
        kt_s[rows, :] = (k_ref[rows, :].astype(F32) * jnp.exp(-bcum)).astype(BF16)
    for t0 in range(0, tb, c):
        rows = slice(t0, t0 + c)
        b_last = bc_s[t0 + c - 1:t0 + c, :]
        ksc_s[rows, :] = (k_ref[rows, :].astype(F32) * jnp.exp(b_last - bc_s[rows, :])).astype(BF16)

    og = og_ref[...]
    cmask = (lax.broadcasted_iota(jnp.int32, (c, c), 1) <= lax.broadcasted_iota(jnp.int32, (c, c), 0))
    for hd in range(GLA_HEADS):
        ks = slice(hd * GLA_DK, (hd + 1) * GLA_DK)
        vs = slice(hd * GLA_DV, (hd + 1) * GLA_DV)
        st = state[hd]
        for t0 in range(0, tb, c):
            rows = slice(t0, t0 + c)
            qh = qt_s[rows, ks]
            vh = v_ref[rows, vs]
            att = jnp.where(cmask, _dot_nt(qh, kt_s[rows, ks]), 0.0)
            o = _dot(att.astype(BF16), vh) + _dot_nt(qh, st.astype(BF16))
            st = st * jnp.exp(bc_s[t0 + c - 1:t0 + c, ks]) + _dot_tn(vh, ksc_s[rows, ks])
            o = o * lax.rsqrt(jnp.mean(o * o, axis=-1, keepdims=True) + NORM_EPS) * og[:, vs]
            rr = r_ref[rows, vs].astype(F32)
            o_ref[rows, vs] = (o * (rr * _sigmoid(rr))).astype(o_ref.dtype)
        state[hd] = st


def _gla(main, glr, gate_w, gate_b, onorm_g, batch, seq):
    t = main.shape[0]
    nt = seq // GLA_TB
    nk = GLA_HEADS * GLA_DK
    nv = GLA_HEADS * GLA_DV
    gw = jnp.concatenate([gate_w, jnp.zeros((LANES - GLA_GATE_RANK, nk), F32)], axis=0)
    const = lambda b, i: (0, 0)
    row = lambda b, i: b * nt + i
    return pl.pallas_call(
        _gla_kernel,
        grid=(batch, nt),
        in_specs=[
            pl.BlockSpec((GLA_TB, nk), lambda b, i: (row(b, i), 0)),
            pl.BlockSpec((GLA_TB, nk), lambda b, i: (row(b, i), 1)),
            pl.BlockSpec((GLA_TB, nv), lambda b, i: (row(b, i), 1)),
            pl.BlockSpec((GLA_TB, nv), lambda b, i: (row(b, i), 2)),
            pl.BlockSpec((GLA_TB, LANES), lambda b, i: (row(b, i), 0)),
            pl.BlockSpec((LANES, nk), const),
            pl.BlockSpec((1, nk), const),
            pl.BlockSpec((1, nv), const),
        ],
        out_specs=pl.BlockSpec((GLA_TB, nv), lambda b, i: (row(b, i), 0)),
        out_shape=jax.ShapeDtypeStruct((t, nv), BF16),
        scratch_shapes=[pltpu.VMEM((GLA_HEADS, GLA_DV, GLA_DK), F32),
                        pltpu.VMEM((GLA_TB, nk), F32),
                        pltpu.VMEM((GLA_TB, nk), BF16),
                        pltpu.VMEM((GLA_TB, nk), BF16),
                        pltpu.VMEM((GLA_TB, nk), BF16)],
        compiler_params=pltpu.CompilerParams(
            dimension_semantics=("arbitrary", "arbitrary"), vmem_limit_bytes=VMEM_LIMIT),
    )(main, main, main, main, glr, gw, gate_b.reshape(1, nk), onorm_g.reshape(1, nv))


def _out_ffn_kernel(*refs, y_widths, final):
    n_y = len(y_widths)
    h_ref = refs[0]
    y_refs = refs[1:1 + n_y]
    wo_ref, g_ref, wg_ref, wu_ref, wd_ref, fg_ref, o_ref, a_scr = refs[1 + n_y:]

    acc = h_ref[...]
    k0 = 0
    for y_ref, kw in zip(y_refs, y_widths):
        acc = acc + _dot(y_ref[...], wo_ref[k0:k0 + kw, :])
        k0 += kw
    o_ref[...] = acc
    hn = _rms(acc, g_ref[...]).astype(BF16)
    for f0 in range(0, D_FF, FFN_FC):
        gt = _dot(hn, wg_ref[:, f0:f0 + FFN_FC])
        up = _dot(hn, wu_ref[:, f0:f0 + FFN_FC])
        a_scr[:, f0:f0 + FFN_FC] = (gt * _sigmoid(gt) * up).astype(BF16)
    out = o_ref[...] + _dot(a_scr[...], wd_ref[...])
    if final:
        out = _rms(out, fg_ref[...])
    o_ref[...] = out


def _out_ffn(h, ys, w_out, g, w_gate, w_up, w_down, final_g, final):
    t = h.shape[0]
    y_widths = tuple(y.shape[1] for y in ys)
    const = lambda i: (0, 0)
    kern = functools.partial(_out_ffn_kernel, y_widths=y_widths, final=final)
    resident = lambda shape: pl.BlockSpec(shape, const, pipeline_mode=pl.Buffered(1))
    return pl.pallas_call(
        kern,
        grid=(t // FFN_TM,),
        in_specs=(
            [pl.BlockSpec((FFN_TM, D_MODEL), lambda i: (i, 0))]
            + [pl.BlockSpec((FFN_TM, kw), lambda i: (i, 0)) for kw in y_widths]
            + [resident((sum(y_widths), D_MODEL)),
               pl.BlockSpec((1, D_MODEL), const),
               resident((D_MODEL, D_FF)),
               resident((D_MODEL, D_FF)),
               resident((D_FF, D_MODEL)),
               pl.BlockSpec((1, D_MODEL), const)]),
        out_specs=pl.BlockSpec((FFN_TM, D_MODEL), lambda i: (i, 0)),
        out_shape=jax.ShapeDtypeStruct((t, D_MODEL), F32),
        scratch_shapes=[pltpu.VMEM((FFN_TM, D_FF), BF16)],
        compiler_params=pltpu.CompilerParams(
            dimension_semantics=("arbitrary",), vmem_limit_bytes=VMEM_LIMIT),
    )(h, *ys, w_out.astype(BF16), g.reshape(1, D_MODEL), w_gate.astype(BF16),
      w_up.astype(BF16), w_down.astype(BF16), final_g.reshape(1, D_MODEL))


def kernel(x, norm_mix_g, ab_w_in, ab_conv_w, ab_conv_b, ab_ln_g, ab_ln_b, ab_w_out,
           c_w_in, c_gate_w, c_gate_b, c_onorm_g, c_w_out,
           norm_ffn_g, ffn_w_gate, ffn_w_up, ffn_w_down, final_norm_g):
    batch, seq, _ = x.shape
    t = batch * seq
    h = x.reshape(t, D_MODEL)
    nh = ATT_HEADS * HEAD_DIM

    w = ab_w_in[0]
    o_q = 2 * CONV_CH
    o_k = o_q + nh
    o_v = o_k + HEAD_DIM
    o_iq = o_v + HEAD_DIM
    o_ik = o_iq + IDX_HEADS * HEAD_DIM
    o_iw = o_ik + HEAD_DIM
    w_k, w_v, w_ik = w[:, o_k:o_v], w[:, o_v:o_iq], w[:, o_ik:o_iw]
    w_iw = jnp.concatenate([w[:, o_iw:], jnp.zeros((D_MODEL, LANES - IDX_HEADS), F32)], axis=1)
    w_all = jnp.concatenate(
        [w[:, :o_k], w[:, o_iq:o_ik], w_k, w_k, w_ik, w_ik, w_v, w_v, w_iw], axis=1).astype(BF16)
    n_main = 2 * CONV_CH + 2 * nh
    main, small = _norm_proj(h, norm_mix_g[0], w_all,
                             ((0, n_main), (n_main, n_main + 4 * LANES)), (BF16, F32))
    ya = _conv_module(main, ab_conv_w[0], ab_conv_b[0], ab_ln_g[0], ab_ln_b[0], batch, seq)
    yb = _dsa(main, small, _rope_lane_tables(seq), batch, seq)
    h = _out_ffn(h, (ya, yb), ab_w_out[0], norm_ffn_g[0], ffn_w_gate[0], ffn_w_up[0],
                 ffn_w_down[0], final_norm_g, False)

    w = c_w_in[0]
    n_main = 2 * GLA_HEADS * GLA_DK + 2 * GLA_HEADS * GLA_DV
    w_all = jnp.concatenate(
        [w, jnp.zeros((D_MODEL, LANES - GLA_GATE_RANK), F32)], axis=1).astype(BF16)
    main, glr = _norm_proj(h, norm_mix_g[1], w_all,
                           ((0, n_main), (n_main, n_main + LANES)), (BF16, F32))
    yc = _gla(main, glr, c_gate_w[0], c_gate_b[0], c_onorm_g[0], batch, seq)
    h = _out_ffn(h, (yc,), c_w_out[0], norm_ffn_g[1], ffn_w_gate[1], ffn_w_up[1],
                 ffn_w_down[1], final_norm_g, True)
    return h.reshape(batch, seq, D_MODEL)
```

```python
import functools

import jax
import jax.numpy as jnp
from jax import lax
from jax.experimental import pallas as pl
from jax.experimental.pallas import tpu as pltpu

F32 = jnp.float32
BF16 = jnp.bfloat16

D_MODEL = 1024
CONV_CH = 512
CONV_WIDTH = 31
ATT_HEADS = 8
HEAD_DIM = 64
IDX_HEADS = 8
IDX_TOPK = 256
GLA_HEADS = 4
GLA_DK = 128
GLA_DV = 256
GLA_GATE_RANK = 16
GLA_TAU = 16.0
GLA_CHUNK = 64
ROPE_THETA = 500000.0
ROPE_ROT = HEAD_DIM // 4
D_FF = 2816
NORM_EPS = 1e-6

LANES = 128
SUBLANES = 8
VMEM_LIMIT = 56 * 1024 * 1024

PROJ_TM = 512
FFN_TM = 512
FFN_FC = 256
CONV_TS = 256
CONV_RC = 32
CONV_HALO = 32
DSA_TQ = 256
GLA_TB = 512
GLA_SUB = 256
BF16_ROWS = 16
F32_EPS = 2.0 ** -23
TOPK_LEVELS = 4
TOPK_MAX_FIX_ROUNDS = 400
TIE_CHUNK = 256


def _sigmoid(x):
    return 1.0 / (1.0 + jnp.exp(-x))


def _rms(x, g):
    ms = jnp.mean(x * x, axis=-1, keepdims=True)
    return x * lax.rsqrt(ms + NORM_EPS) * g


def _dot(a, b):
    return jnp.dot(a, b, preferred_element_type=F32)


def _dot_nt(a, b):
    return lax.dot_general(a, b, (((1,), (1,)), ((), ())), preferred_element_type=F32)


def _dot_tn(a, b):
    return lax.dot_general(a, b, (((0,), (0,)), ((), ())), preferred_element_type=F32)


def _split3(x):
    hi = x.astype(BF16)
    r1 = x - hi.astype(F32)
    mid = r1.astype(BF16)
    lo = (r1 - mid.astype(F32)).astype(BF16)
    return hi, mid, lo


def _norm_proj_kernel(x_ref, g_ref, w_ref, *out_refs, groups, nchunk):
    xn = _rms(x_ref[...], g_ref[...]).astype(BF16)
    for (c0, c1), o_ref in zip(groups, out_refs):
        for n0 in range(c0, c1, nchunk):
            n1 = min(n0 + nchunk, c1)
            o_ref[:, n0 - c0:n1 - c0] = _dot(xn, w_ref[:, n0:n1]).astype(o_ref.dtype)


def _norm_proj(h, g, w, groups, dtypes):
    t = h.shape[0]
    n = w.shape[1]
    kern = functools.partial(_norm_proj_kernel, groups=groups, nchunk=512)
    return pl.pallas_call(
        kern,
        grid=(t // PROJ_TM,),
        in_specs=[
            pl.BlockSpec((PROJ_TM, D_MODEL), lambda i: (i, 0)),
            pl.BlockSpec((1, D_MODEL), lambda i: (0, 0)),
            pl.BlockSpec((D_MODEL, n), lambda i: (0, 0)),
        ],
        out_specs=[pl.BlockSpec((PROJ_TM, c1 - c0), lambda i: (i, 0)) for c0, c1 in groups],
        out_shape=[jax.ShapeDtypeStruct((t, c1 - c0), dt) for (c0, c1), dt in zip(groups, dtypes)],
        compiler_params=pltpu.CompilerParams(
            dimension_semantics=("arbitrary",), vmem_limit_bytes=VMEM_LIMIT),
    )(h, g.reshape(1, D_MODEL), w)


def _conv_stages(ua_ref, r_in, w_ref, cb, lg, lb, o_ref, r_out, hbuf, ysh, fresh):
    ts = CONV_TS
    base = CONV_HALO - (CONV_WIDTH - 1)

    def prepare():
        tail = hbuf[ts:ts + CONV_HALO, :]
        hbuf[0:CONV_HALO, :] = jnp.where(fresh, 0.0, tail)
        a = ua_ref[r_in:r_in + ts, 0:CONV_CH].astype(F32)
        gate = ua_ref[r_in:r_in + ts, CONV_CH:2 * CONV_CH].astype(F32)
        hbuf[CONV_HALO:CONV_HALO + ts, :] = a * _sigmoid(gate)
        ylen = ysh.shape[1]
        for b in range(1, SUBLANES):
            ysh[b - 1] = hbuf[b:b + ylen, :]
        return None

    def rows(r):
        acc = jnp.zeros((CONV_RC, CONV_CH), F32)
        for s in range(base, base + CONV_WIDTH):
            a8, b = divmod(s, SUBLANES)
            r8 = r + a8 * SUBLANES
            tap = hbuf[r8:r8 + CONV_RC, :] if b == 0 else ysh[b - 1, r8:r8 + CONV_RC, :]
            acc = acc + tap * w_ref[s - base:s - base + 1, :]
        acc = acc + cb
        mu = jnp.mean(acc, axis=-1, keepdims=True)
        d = acc - mu
        var = jnp.mean(d * d, axis=-1, keepdims=True)
        y = d * lax.rsqrt(var + NORM_EPS) * lg + lb
        o_ref[r_out + r:r_out + r + CONV_RC, :] = (y * _sigmoid(y)).astype(o_ref.dtype)
        return y[0:BF16_ROWS, 0:LANES]

    return [prepare] + [functools.partial(rows, r) for r in range(0, ts, CONV_RC)]


def _rope_group(xg, c, a, bm):
    return xg * c + pltpu.roll(xg, LANES - ROPE_ROT // 2, 1) * a + pltpu.roll(xg, ROPE_ROT // 2, 1) * bm


def _count_ge(x, c):
    return jnp.sum(jnp.where(x >= c, 1.0, 0.0), axis=0, keepdims=True)


def _count_ge_packed(db_ref, kv, cand):
    tq = db_ref.shape[1]
    cand_b = jnp.broadcast_to(cand, (BF16_ROWS, tq)).astype(BF16)
    one = jnp.ones((BF16_ROWS, tq), BF16)
    zero = jnp.zeros((BF16_ROWS, tq), BF16)
    parts = [jnp.where(db_ref[r:r + BF16_ROWS, :] >= cand_b, one, zero)
             for r in range(0, kv, BF16_ROWS)]
    while len(parts) > 1:
        nxt = [a + b for a, b in zip(parts[0::2], parts[1::2])]
        if len(parts) % 2:
            nxt.append(parts[-1])
        parts = nxt
    return jnp.sum(parts[0].astype(F32), axis=0, keepdims=True)


def _select_topk(sc_s, db_s, bt_s, kv, kf, causal, few):
    tq = sc_s.shape[1]
    neg_inf = -jnp.inf
    s_all = sc_s[0:kv, :]
    mx = jnp.max(s_all, axis=0, keepdims=True)
    mn = jnp.min(jnp.where(causal, s_all, jnp.inf), axis=0, keepdims=True)
    hi_top = mx + jnp.maximum((mx - mn) * (1.0 / 64.0), jnp.abs(mx) * (2.0 ** -20) + 1e-30)

    lo, hi = mn, hi_top
    for _ in range(TOPK_LEVELS):
        ulp = jnp.maximum(jnp.maximum(jnp.abs(lo), jnp.abs(hi)) * F32_EPS, 1e-30)
        step = jnp.maximum((hi - lo) * (1.0 / 256.0), ulp)
        inv = 1.0 / step
        db_s[0:kv, :] = jnp.clip((sc_s[0:kv, :] - lo) * inv, -1.0, 257.0).astype(BF16)

        def digit(_, carry):
            c, bit = carry
            cand = c + bit
            cnt = _count_ge_packed(db_s, kv, cand)
            return jnp.where(cnt >= kf, cand, c), bit * 0.5

        c, _ = lax.fori_loop(0, 8, digit, (jnp.zeros((1, tq), F32), jnp.full((1, tq), 128.0, F32)))
        lo, hi = lo + (c - 1.0) * step, lo + (c + 2.0) * step

    def verify(hi):
        s = sc_s[0:kv, :]
        below = s < hi
        vstar = jnp.max(jnp.where(below, s, neg_inf), axis=0, keepdims=True)
        c_hi = float(kv) - jnp.sum(jnp.where(below, 1.0, 0.0), axis=0, keepdims=True)
        return vstar, c_hi, _count_ge(s, vstar)

    def is_ok(c_hi, c_ge):
        ok = jnp.logical_and(c_hi < kf, c_ge >= kf)
        return ok if few is None else jnp.logical_or(ok, few)

    def unfinished(st):
        _, _, _, _, c_hi, c_ge, n = st
        bad = jnp.max(jnp.where(is_ok(c_hi, c_ge), 0.0, 1.0))
        return jnp.logical_and(bad > 0.5, n < TOPK_MAX_FIX_ROUNDS)

    def fix(st):
        lo, hi, span, vstar, c_hi, c_ge, n = st
        bad = jnp.logical_not(is_ok(c_hi, c_ge))
        hi_true = jnp.logical_and(bad, c_hi >= kf)
        lo = jnp.where(hi_true, hi, lo)
        hi = jnp.where(hi_true, jnp.minimum(hi + span, hi_top),
                       jnp.where(jnp.logical_and(bad, c_ge < kf), vstar, hi))
        span = jnp.where(hi_true, span * 8.0, span)
        mid = 0.5 * lo + 0.5 * hi
        up = _count_ge(sc_s[0:kv, :], mid) >= kf
        lo = jnp.where(jnp.logical_and(bad, up), mid, lo)
        hi = jnp.where(jnp.logical_and(bad, jnp.logical_not(up)), mid, hi)
        vstar, c_hi, c_ge = verify(hi)
        return lo, hi, span, vstar, c_hi, c_ge, n + 1

    vstar, c_hi, c_ge = verify(hi)
    _, _, _, vstar, c_hi, c_ge, _ = lax.while_loop(
        unfinished, fix, (mn, hi, hi - lo, vstar, c_hi, c_ge, jnp.int32(0)))

    need = kf - c_hi
    if few is not None:
        vstar = jnp.where(few, neg_inf, vstar)
        c_ge = jnp.where(few, 0.0, c_ge)
    has_extra_ties = jnp.max(c_ge) > kf

    @pl.when(jnp.logical_not(has_extra_ties))
    def _():
        keep = sc_s[0:kv, :] >= vstar
        if few is not None:
            keep = jnp.logical_and(keep, causal)
        bt_s[0:kv, :] = jnp.where(keep, 0.0, neg_inf)

    @pl.when(has_extra_ties)
    def _():
        ri = lax.broadcasted_iota(jnp.int32, (TIE_CHUNK, TIE_CHUNK), 0)
        ci = lax.broadcasted_iota(jnp.int32, (TIE_CHUNK, TIE_CHUNK), 1)
        tri = jnp.where(ci <= ri, 1.0, 0.0).astype(BF16)
        seen = jnp.zeros((1, tq), F32)
        for c0 in range(0, kv, TIE_CHUNK):
            s = sc_s[c0:c0 + TIE_CHUNK, :]
            eq = jnp.where(s == vstar, 1.0, 0.0)
            rank = _dot(tri, eq.astype(BF16)) + seen
            seen = seen + jnp.sum(eq, axis=0, keepdims=True)
            keep = jnp.logical_or(s > vstar, jnp.logical_and(s == vstar, rank <= need))
            if few is not None:
                keep = jnp.logical_and(keep, causal[c0:c0 + TIE_CHUNK, :])
            bt_s[c0:c0 + TIE_CHUNK, :] = jnp.where(keep, 0.0, neg_inf)


def _dsa_kernel(q_ref, iq_ref, small_ref, tc_ref, ta_ref, tb_ref, o_ref,
                k_s, ik_s, v_s, sc_s, db_s, bt_s, bias_s, *, seq):
    tq = DSA_TQ
    w_scale = (IDX_HEADS ** -0.5) * (HEAD_DIM ** -0.5)
    att_scale = HEAD_DIM ** -0.5
    n_sel = min(IDX_TOPK, seq // 4)
    lane = lax.broadcasted_iota(jnp.int32, (1, LANES), 1)
    lo_half = lane < HEAD_DIM

    tc = tc_ref[...]
    ta = ta_ref[...]
    tb = tb_ref[...]
    k_s[...] = _rope_group(small_ref[:, 0:LANES], tc, ta, tb).astype(BF16)
    ik_s[...] = _rope_group(small_ref[:, LANES:2 * LANES], tc, ta, tb).astype(BF16)
    v_s[...] = small_ref[:, 2 * LANES:3 * LANES].astype(BF16)

    def rope_split(ref, r0, g, scale=None):
        xg = ref[r0:r0 + tq, g * LANES:(g + 1) * LANES].astype(F32)
        xr = _rope_group(xg, tc_ref[r0:r0 + tq, :], ta_ref[r0:r0 + tq, :], tb_ref[r0:r0 + tq, :])
        if scale is not None:
            xr = xr * scale
        return (jnp.where(lo_half, xr, 0.0).astype(BF16), jnp.where(lo_half, 0.0, xr).astype(BF16))

    for blk in range(seq // tq):
        r0 = blk * tq
        kv = r0 + tq
        iw_t = jnp.transpose(small_ref[r0:r0 + tq, 3 * LANES:4 * LANES])
        ik = ik_s[0:kv, :]
        score = jnp.zeros((kv, tq), F32)
        for g in range(IDX_HEADS // 2):
            halves = rope_split(iq_ref, r0, g)
            for par in range(2):
                hh = 2 * g + par
                logits = _dot_nt(ik, halves[par])
                score = score + jnp.maximum(logits, 0.0) * iw_t[hh:hh + 1, :]
        score = score * w_scale
        key_pos = lax.broadcasted_iota(jnp.int32, (kv, tq), 0)
        q_pos = r0 + lax.broadcasted_iota(jnp.int32, (kv, tq), 1)
        causal = key_pos <= q_pos
        sc_s[0:kv, :] = jnp.where(causal, score, -jnp.inf)

        if kv <= n_sel:
            bt_s[0:kv, :] = jnp.where(
                jnp.logical_and(causal, sc_s[0:kv, :] >= -jnp.inf), 0.0, -jnp.inf)
        else:
            few = None
            if r0 < n_sel:
                few = r0 + lax.broadcasted_iota(jnp.int32, (1, tq), 1) < n_sel
            _select_topk(sc_s, db_s, bt_s, kv, float(n_sel), causal, few)
        bias_s[:, 0:kv] = jnp.transpose(bt_s[0:kv, :])

        kk = k_s[0:kv, :]
        vv = v_s[0:kv, :]
        for g in range(ATT_HEADS // 2):
            halves = rope_split(q_ref, r0, g, att_scale)
            outs = []
            for par in range(2):
                s = _dot_nt(halves[par], kk) + bias_s[:, 0:kv]
                m = jnp.max(s, axis=1, keepdims=True)
                p = jnp.exp(s - m)
                l = jnp.sum(p, axis=1, keepdims=True)
                outs.append(_dot(p.astype(BF16), vv) / l)
            o_ref[r0:r0 + tq, g * LANES:(g + 1) * LANES] = (
                jnp.where(lo_half, outs[0], outs[1]).astype(o_ref.dtype))


def _dsa(main, small, tabs, batch, seq):
    t = main.shape[0]
    nh = ATT_HEADS * HEAD_DIM
    const = lambda b: (0, 0)
    kern = functools.partial(_dsa_kernel, seq=seq)
    return pl.pallas_call(
        kern,
        grid=(batch,),
        in_specs=[
            pl.BlockSpec((seq, nh), lambda b: (b, 2)),
            pl.BlockSpec((seq, nh), lambda b: (b, 3)),
            pl.BlockSpec((seq, 4 * LANES), lambda b: (b, 0)),
            pl.BlockSpec((seq, LANES), const),
            pl.BlockSpec((seq, LANES), const),
            pl.BlockSpec((seq, LANES), const),
        ],
        out_specs=pl.BlockSpec((seq, nh), lambda b: (b, 0)),
        out_shape=jax.ShapeDtypeStruct((t, nh), BF16),
        scratch_shapes=[
            pltpu.VMEM((seq, LANES), BF16),
            pltpu.VMEM((seq, LANES), BF16),
            pltpu.VMEM((seq, LANES), BF16),
            pltpu.VMEM((seq, DSA_TQ), F32),
            pltpu.VMEM((seq, DSA_TQ), BF16),
            pltpu.VMEM((seq, DSA_TQ), F32),
            pltpu.VMEM((DSA_TQ, seq), F32),
        ],
        compiler_params=pltpu.CompilerParams(
            dimension_semantics=("arbitrary",), vmem_limit_bytes=VMEM_LIMIT),
    )(main, main, small, *tabs)


def _rope_lane_tables(seq):
    half = ROPE_ROT // 2
    inv = ROPE_THETA ** (-jnp.arange(0, ROPE_ROT, 2, dtype=F32) / ROPE_ROT)
    ang = jnp.arange(seq, dtype=F32)[:, None] * inv[None, :]
    cos, sin = jnp.cos(ang), jnp.sin(ang)
    rest = HEAD_DIM - ROPE_ROT
    c64 = jnp.concatenate([cos, cos, jnp.ones((seq, rest), F32)], axis=1)
    a64 = jnp.concatenate([-sin, jnp.zeros((seq, HEAD_DIM - half), F32)], axis=1)
    b64 = jnp.concatenate([jnp.zeros((seq, half), F32), sin, jnp.zeros((seq, rest), F32)], axis=1)
    return tuple(jnp.tile(x, (1, LANES // HEAD_DIM)) for x in (c64, a64, b64))


def _gla_kernel(q_ref, k_ref, v_ref, r_ref, glr_ref, gw_ref, gb_ref, og_ref, o_ref,
                state, bc_s, qt_s, kt_s, ksc_s):
    tb, c, sb = GLA_TB, GLA_CHUNK, GLA_SUB

    @pl.when(pl.program_id(1) == 0)
    def _():
        state[...] = jnp.zeros(state.shape, F32)

    ri = lax.broadcasted_iota(jnp.int32, (sb, sb), 0)
    ci = lax.broadcasted_iota(jnp.int32, (sb, sb), 1)
    shift = c.bit_length() - 1
    tri = jnp.logical_and(ci <= ri, lax.shift_right_logical(ri, shift) == lax.shift_right_logical(ci, shift))
    tri_b = jnp.where(tri, 1.0, 0.0).astype(BF16)
    gw_h, gw_m, gw_l = _split3(gw_ref[...])
    gb = gb_ref[...]

    for s0 in range(0, tb, sb):
        rows = slice(s0, s0 + sb)
        gl_h, gl_m, gl_l = _split3(glr_ref[rows, :])
        pre = (_dot(gl_h, gw_h) + (_dot(gl_h, gw_m) + _dot(gl_m, gw_h))
               + (_dot(gl_h, gw_l) + _dot(gl_m, gw_m) + _dot(gl_l, gw_h))) + gb
        log_a = (jnp.minimum(pre, 0.0) - jnp.log(1.0 + jnp.exp(-jnp.abs(pre)))) / GLA_TAU
        la_h, la_m, la_l = _split3(log_a)
        bcum = _dot(tri_b, la_h) + _dot(tri_b, la_m) + _dot(tri_b, la_l)
        bc_s[rows, :] = bcum
        qt_s[rows, :] = (q_ref[rows, :].astype(F32) * (GLA_DK ** -0.5) * jnp.exp(bcum)).astype(BF16)
        kt_s[rows, :] = (k_ref[rows, :].astype(F32) * jnp.exp(-bcum)).astype(BF16)
    for t0 in range(0, tb, c):
        rows = slice(t0, t0 + c)
        b_last = bc_s[t0 + c - 1:t0 + c, :]
        ksc_s[rows, :] = (k_ref[rows, :].astype(F32) * jnp.exp(b_last - bc_s[rows, :])).astype(BF16)

    og = og_ref[...]
    cmask = (lax.broadcasted_iota(jnp.int32, (c, c), 1) <= lax.broadcasted_iota(jnp.int32, (c, c), 0))
    for hd in range(GLA_HEADS):
        ks = slice(hd * GLA_DK, (hd + 1) * GLA_DK)
        vs = slice(hd * GLA_DV, (hd + 1) * GLA_DV)
        st = state[hd]
        for t0 in range(0, tb, c):
            rows = slice(t0, t0 + c)
            qh = qt_s[rows, ks]
            vh = v_ref[rows, vs]
            att = jnp.where(cmask, _dot_nt(qh, kt_s[rows, ks]), 0.0)
            o = _dot(att.astype(BF16), vh) + _dot_nt(qh, st.astype(BF16))
            st = st * jnp.exp(bc_s[t0 + c - 1:t0 + c, ks]) + _dot_tn(vh, ksc_s[rows, ks])
            o = o * lax.rsqrt(jnp.mean(o * o, axis=-1, keepdims=True) + NORM_EPS) * og[:, vs]
            rr = r_ref[rows, vs].astype(F32)
            o_ref[rows, vs] = (o * (rr * _sigmoid(rr))).astype(o_ref.dtype)
        state[hd] = st


def _gla(main, glr, gate_w, gate_b, onorm_g, batch, seq):
    t = main.shape[0]
    nt = seq // GLA_TB
    nk = GLA_HEADS * GLA_DK
    nv = GLA_HEADS * GLA_DV
    gw = jnp.concatenate([gate_w, jnp.zeros((LANES - GLA_GATE_RANK, nk), F32)], axis=0)
    const = lambda b, i: (0, 0)
    row = lambda b, i: b * nt + i
    return pl.pallas_call(
        _gla_kernel,
        grid=(batch, nt),
        in_specs=[
            pl.BlockSpec((GLA_TB, nk), lambda b, i: (row(b, i), 0)),
            pl.BlockSpec((GLA_TB, nk), lambda b, i: (row(b, i), 1)),
            pl.BlockSpec((GLA_TB, nv), lambda b, i: (row(b, i), 1)),
            pl.BlockSpec((GLA_TB, nv), lambda b, i: (row(b, i), 2)),
            pl.BlockSpec((GLA_TB, LANES), lambda b, i: (row(b, i), 0)),
            pl.BlockSpec((LANES, nk), const),
            pl.BlockSpec((1, nk), const),
            pl.BlockSpec((1, nv), const),
        ],
        out_specs=pl.BlockSpec((GLA_TB, nv), lambda b, i: (row(b, i), 0)),
        out_shape=jax.ShapeDtypeStruct((t, nv), BF16),
        scratch_shapes=[pltpu.VMEM((GLA_HEADS, GLA_DV, GLA_DK), F32),
                        pltpu.VMEM((GLA_TB, nk), F32),
                        pltpu.VMEM((GLA_TB, nk), BF16),
                        pltpu.VMEM((GLA_TB, nk), BF16),
                        pltpu.VMEM((GLA_TB, nk), BF16)],
        compiler_params=pltpu.CompilerParams(
            dimension_semantics=("arbitrary", "arbitrary"), vmem_limit_bytes=VMEM_LIMIT),
    )(main, main, main, main, glr, gw, gate_b.reshape(1, nk), onorm_g.reshape(1, nv))


def _out_ffn_kernel(*refs, y_widths, final, conv_tiles_per_seq):
    fused_conv = conv_tiles_per_seq is not None
    n_y = len(y_widths) - (1 if fused_conv else 0)
    h_ref = refs[0]
    y_refs = list(refs[1:1 + n_y])
    rest = refs[1 + n_y:]
    if fused_conv:
        (ua_ref, cw_ref, cb_ref, lg_ref, lb_ref, wo_ref, g_ref, wg_ref, wu_ref, wd_ref, fg_ref,
         o_ref, a_scr, ya_s, ya_next, hbuf, ysh) = rest
        step = pl.program_id(0)

        @pl.when(step == 0)
        def _():
            hbuf[...] = jnp.zeros(hbuf.shape, F32)
            ya_s[...] = jnp.zeros(ya_s.shape, BF16)

        y_refs = [ya_s] + y_refs
    else:
        wo_ref, g_ref, wg_ref, wu_ref, wd_ref, fg_ref, o_ref, a_scr = rest

    side = []
    if fused_conv:
        cb, lg, lb = cb_ref[...], lg_ref[...], lb_ref[...]
        seq_start = lax.rem(step, conv_tiles_per_seq) == 0
        for r in range(0, FFN_TM, CONV_TS):
            fresh = seq_start if r == 0 else False
            side += _conv_stages(ua_ref, r, cw_ref, cb, lg, lb, ya_next, r, hbuf, ysh, fresh)
    n_chunks = D_FF // FFN_FC

    def run_side(slot):
        lo, hi = (len(side) * slot) // n_chunks, (len(side) * (slot + 1)) // n_chunks
        toks = [t for t in (fn() for fn in side[lo:hi]) if t is not None]
        if not toks:
            return None
        bits = pltpu.bitcast(functools.reduce(lambda a, b: a + b, toks), jnp.uint32)
        zero = lax.shift_right_logical(lax.shift_right_logical(bits, jnp.uint32(16)), jnp.uint32(16))
        return pltpu.bitcast(zero, F32).astype(BF16)

    acc = h_ref[...]
    k0 = 0
    for y_ref, kw in zip(y_refs, y_widths):
        acc = acc + _dot(y_ref[...], wo_ref[k0:k0 + kw, :])
        k0 += kw
    o_ref[...] = acc
    hn = _rms(acc, g_ref[...]).astype(BF16)
    for ci in range(n_chunks):
        f0 = ci * FFN_FC
        gt = _dot(hn, wg_ref[:, f0:f0 + FFN_FC])
        up = _dot(hn, wu_ref[:, f0:f0 + FFN_FC])
        a_scr[:, f0:f0 + FFN_FC] = (gt * _sigmoid(gt) * up).astype(BF16)
        zero = run_side(ci)
        if zero is not None:
            a_scr[0:BF16_ROWS, f0:f0 + LANES] = a_scr[0:BF16_ROWS, f0:f0 + LANES] + zero
    out = o_ref[...] + _dot(a_scr[...], wd_ref[...])
    if final:
        out = _rms(out, fg_ref[...])
    o_ref[...] = out
    if fused_conv:
        ya_s[...] = ya_next[...]


def _out_ffn(h, ys, w_out, g, w_gate, w_up, w_down, final_g, final, conv=None):
    t = h.shape[0]
    n = t // FFN_TM
    const = lambda i: (0, 0)
    resident = lambda shape: pl.BlockSpec(shape, const, pipeline_mode=pl.Buffered(1))
    y_widths = tuple(y.shape[1] for y in ys)
    scratch = [pltpu.VMEM((FFN_TM, D_FF), BF16)]
    if conv is None:
        grid, row, conv_specs, conv_args, tiles_per_seq = (n,), (lambda i: (i, 0)), [], [], None
    else:
        main, conv_w, conv_b, ln_g, ln_b, seq = conv
        grid, row = (n + 1,), (lambda i: (jnp.maximum(i - 1, 0), 0))
        tiles_per_seq = seq // FFN_TM
        y_widths = (CONV_CH,) + y_widths
        cw = jnp.concatenate([conv_w.reshape(CONV_WIDTH, CONV_CH), jnp.zeros((1, CONV_CH), F32)], axis=0)
        vec = lambda v: v.reshape(1, CONV_CH)
        conv_specs = [pl.BlockSpec((FFN_TM, 2 * CONV_CH), lambda i: (jnp.minimum(i, n - 1), 0)),
                      pl.BlockSpec((CONV_WIDTH + 1, CONV_CH), const),
                      pl.BlockSpec((1, CONV_CH), const),
                      pl.BlockSpec((1, CONV_CH), const),
                      pl.BlockSpec((1, CONV_CH), const)]
        conv_args = [main, cw, vec(conv_b), vec(ln_g), vec(ln_b)]
        scratch += [pltpu.VMEM((FFN_TM, CONV_CH), BF16),
                    pltpu.VMEM((FFN_TM, CONV_CH), BF16),
                    pltpu.VMEM((CONV_HALO + CONV_TS, CONV_CH), F32),
                    pltpu.VMEM((SUBLANES - 1, CONV_TS + CONV_HALO - SUBLANES, CONV_CH), F32)]
    kern = functools.partial(_out_ffn_kernel, y_widths=y_widths, final=final,
                             conv_tiles_per_seq=tiles_per_seq)
    return pl.pallas_call(
        kern,
        grid=grid,
        in_specs=(
            [pl.BlockSpec((FFN_TM, D_MODEL), row)]
            + [pl.BlockSpec((FFN_TM, y.shape[1]), row) for y in ys]
            + conv_specs
            + [resident((sum(y_widths), D_MODEL)),
               pl.BlockSpec((1, D_MODEL), const),
               resident((D_MODEL, D_FF)),
               resident((D_MODEL, D_FF)),
               resident((D_FF, D_MODEL)),
               pl.BlockSpec((1, D_MODEL), const)]),
        out_specs=pl.BlockSpec((FFN_TM, D_MODEL), row),
        out_shape=jax.ShapeDtypeStruct((t, D_MODEL), F32),
        scratch_shapes=scratch,
        compiler_params=pltpu.CompilerParams(
            dimension_semantics=("arbitrary",), vmem_limit_bytes=VMEM_LIMIT),
    )(h, *ys, *conv_args, w_out.astype(BF16), g.reshape(1, D_MODEL), w_gate.astype(BF16),
      w_up.astype(BF16), w_down.astype(BF16), final_g.reshape(1, D_MODEL))


def kernel(x, norm_mix_g, ab_w_in, ab_conv_w, ab_conv_b, ab_ln_g, ab_ln_b, ab_w_out,
           c_w_in, c_gate_w, c_gate_b, c_onorm_g, c_w_out,
           norm_ffn_g, ffn_w_gate, ffn_w_up, ffn_w_down, final_norm_g):
    batch, seq, _ = x.shape
    t = batch * seq
    h = x.reshape(t, D_MODEL)
    nh = ATT_HEADS * HEAD_DIM

    w = ab_w_in[0]
    o_q = 2 * CONV_CH
    o_k = o_q + nh
    o_v = o_k + HEAD_DIM
    o_iq = o_v + HEAD_DIM
    o_ik = o_iq + IDX_HEADS * HEAD_DIM
    o_iw = o_ik + HEAD_DIM
    w_k, w_v, w_ik = w[:, o_k:o_v], w[:, o_v:o_iq], w[:, o_ik:o_iw]
    w_iw = jnp.concatenate([w[:, o_iw:], jnp.zeros((D_MODEL, LANES - IDX_HEADS), F32)], axis=1)
    w_all = jnp.concatenate(
        [w[:, :o_k], w[:, o_iq:o_ik], w_k, w_k, w_ik, w_ik, w_v, w_v, w_iw], axis=1).astype(BF16)
    n_main = 2 * CONV_CH + 2 * nh
    main, small = _norm_proj(h, norm_mix_g[0], w_all,
                             ((0, n_main), (n_main, n_main + 4 * LANES)), (BF16, F32))
    yb = _dsa(main, small, _rope_lane_tables(seq), batch, seq)
    h = _out_ffn(h, (yb,), ab_w_out[0], norm_ffn_g[0], ffn_w_gate[0], ffn_w_up[0],
                 ffn_w_down[0], final_norm_g, False,
                 conv=(main, ab_conv_w[0], ab_conv_b[0], ab_ln_g[0], ab_ln_b[0], seq))

    w = c_w_in[0]
    n_main = 2 * GLA_HEADS * GLA_DK + 2 * GLA_HEADS * GLA_DV
    w_all = jnp.concatenate(
        [w, jnp.zeros((D_MODEL, LANES - GLA_GATE_RANK), F32)], axis=1).astype(BF16)
    main, glr = _norm_proj(h, norm_mix_g[1], w_all,
                           ((0, n_main), (n_main, n_main + LANES)), (BF16, F32))
    yc = _gla(main, glr, c_gate_w[0], c_gate_b[0], c_onorm_g[0], batch, seq)
    h = _out_ffn(h, (yc,), c_w_out[0], norm_ffn_g[1], ffn_w_gate[1], ffn_w_up[1],
                 ffn_w_down[1], final_norm_g, True)
    return h.reshape(batch, seq, D_MODEL)
```

```python
import functools

import jax
import jax.numpy as jnp
from jax import lax
from jax.experimental import pallas as pl
from jax.experimental.pallas import tpu as pltpu

F32 = jnp.float32
BF16 = jnp.bfloat16

D_MODEL = 1024
CONV_CH = 512
CONV_WIDTH = 31
ATT_HEADS = 8
HEAD_DIM = 64
IDX_HEADS = 8
IDX_TOPK = 256
GLA_HEADS = 4
GLA_DK = 128
GLA_DV = 256
GLA_GATE_RANK = 16
GLA_TAU = 16.0
GLA_CHUNK = 64
ROPE_THETA = 500000.0
ROPE_ROT = HEAD_DIM // 4
D_FF = 2816
NORM_EPS = 1e-6

LANES = 128
SUBLANES = 8
VMEM_LIMIT = 56 * 1024 * 1024

PROJ_TM = 512
FFN_TM = 512
FFN_FC = 256
CONV_TS = 256
CONV_RC = 32
CONV_HALO = 32
DSA_TQ = 256
GLA_TB = 512
GLA_SUB = 256
BF16_ROWS = 16
F32_EPS = 2.0 ** -23
TOPK_LEVELS = 4
TOPK_MAX_FIX_ROUNDS = 400
TIE_CHUNK = 256


def _sigmoid(x):
    return 1.0 / (1.0 + jnp.exp(-x))


def _rms(x, g):
    ms = jnp.mean(x * x, axis=-1, keepdims=True)
    return x * lax.rsqrt(ms + NORM_EPS) * g


def _dot(a, b):
    return jnp.dot(a, b, preferred_element_type=F32)


def _dot_nt(a, b):
    return lax.dot_general(a, b, (((1,), (1,)), ((), ())), preferred_element_type=F32)


def _dot_tn(a, b):
    return lax.dot_general(a, b, (((0,), (0,)), ((), ())), preferred_element_type=F32)


def _split3(x):
    hi = x.astype(BF16)
    r1 = x - hi.astype(F32)
    mid = r1.astype(BF16)
    lo = (r1 - mid.astype(F32)).astype(BF16)
    return hi, mid, lo


def _norm_proj_kernel(x_ref, g_ref, w_ref, *out_refs, groups, nchunk):
    xn = _rms(x_ref[...], g_ref[...]).astype(BF16)
    for (c0, c1), o_ref in zip(groups, out_refs):
        for n0 in range(c0, c1, nchunk):
            n1 = min(n0 + nchunk, c1)
            o_ref[:, n0 - c0:n1 - c0] = _dot(xn, w_ref[:, n0:n1]).astype(o_ref.dtype)


def _norm_proj(h, g, w, groups, dtypes):
    t = h.shape[0]
    n = w.shape[1]
    kern = functools.partial(_norm_proj_kernel, groups=groups, nchunk=512)
    return pl.pallas_call(
        kern,
        grid=(t // PROJ_TM,),
        in_specs=[
            pl.BlockSpec((PROJ_TM, D_MODEL), lambda i: (i, 0)),
            pl.BlockSpec((1, D_MODEL), lambda i: (0, 0)),
            pl.BlockSpec((D_MODEL, n), lambda i: (0, 0)),
        ],
        out_specs=[pl.BlockSpec((PROJ_TM, c1 - c0), lambda i: (i, 0)) for c0, c1 in groups],
        out_shape=[jax.ShapeDtypeStruct((t, c1 - c0), dt) for (c0, c1), dt in zip(groups, dtypes)],
        compiler_params=pltpu.CompilerParams(
            dimension_semantics=("arbitrary",), vmem_limit_bytes=VMEM_LIMIT),
    )(h, g.reshape(1, D_MODEL), w)


def _conv_stages(ua_ref, r_in, w_ref, cb, lg, lb, o_ref, r_out, hbuf, ysh, fresh):
    ts = CONV_TS
    base = CONV_HALO - (CONV_WIDTH - 1)

    def prepare():
        tail = hbuf[ts:ts + CONV_HALO, :]
        hbuf[0:CONV_HALO, :] = jnp.where(fresh, 0.0, tail)
        a = ua_ref[r_in:r_in + ts, 0:CONV_CH].astype(F32)
        gate = ua_ref[r_in:r_in + ts, CONV_CH:2 * CONV_CH].astype(F32)
        hbuf[CONV_HALO:CONV_HALO + ts, :] = a * _sigmoid(gate)
        ylen = ysh.shape[1]
        for b in range(1, SUBLANES):
            ysh[b - 1] = hbuf[b:b + ylen, :]
        return None

    def rows(r):
        acc = jnp.zeros((CONV_RC, CONV_CH), F32)
        for s in range(base, base + CONV_WIDTH):
            a8, b = divmod(s, SUBLANES)
            r8 = r + a8 * SUBLANES
            tap = hbuf[r8:r8 + CONV_RC, :] if b == 0 else ysh[b - 1, r8:r8 + CONV_RC, :]
            acc = acc + tap * w_ref[s - base:s - base + 1, :]
        acc = acc + cb
        mu = jnp.mean(acc, axis=-1, keepdims=True)
        d = acc - mu
        var = jnp.mean(d * d, axis=-1, keepdims=True)
        y = d * lax.rsqrt(var + NORM_EPS) * lg + lb
        o_ref[r_out + r:r_out + r + CONV_RC, :] = (y * _sigmoid(y)).astype(o_ref.dtype)
        return y[0:BF16_ROWS, 0:LANES]

    return [prepare] + [functools.partial(rows, r) for r in range(0, ts, CONV_RC)]


def _rope_group(xg, c, a, bm):
    return xg * c + pltpu.roll(xg, LANES - ROPE_ROT // 2, 1) * a + pltpu.roll(xg, ROPE_ROT // 2, 1) * bm


def _count_ge(x, c):
    return jnp.sum(jnp.where(x >= c, 1.0, 0.0), axis=0, keepdims=True)


def _count_ge_packed(db_ref, kv, cand):
    tq = db_ref.shape[1]
    cand_b = jnp.broadcast_to(cand, (BF16_ROWS, tq)).astype(BF16)
    one = jnp.ones((BF16_ROWS, tq), BF16)
    zero = jnp.zeros((BF16_ROWS, tq), BF16)
    parts = [jnp.where(db_ref[r:r + BF16_ROWS, :] >= cand_b, one, zero)
             for r in range(0, kv, BF16_ROWS)]
    while len(parts) > 1:
        nxt = [a + b for a, b in zip(parts[0::2], parts[1::2])]
        if len(parts) % 2:
            nxt.append(parts[-1])
        parts = nxt
    return jnp.sum(parts[0].astype(F32), axis=0, keepdims=True)


def _select_topk(sc_s, db_s, bt_s, kv, kf, causal, few):
    tq = sc_s.shape[1]
    neg_inf = -jnp.inf
    s_all = sc_s[0:kv, :]
    mx = jnp.max(s_all, axis=0, keepdims=True)
    mn = jnp.min(jnp.where(causal, s_all, jnp.inf), axis=0, keepdims=True)
    hi_top = mx + jnp.maximum((mx - mn) * (1.0 / 64.0), jnp.abs(mx) * (2.0 ** -20) + 1e-30)

    lo, hi = mn, hi_top
    for _ in range(TOPK_LEVELS):
        ulp = jnp.maximum(jnp.maximum(jnp.abs(lo), jnp.abs(hi)) * F32_EPS, 1e-30)
        step = jnp.maximum((hi - lo) * (1.0 / 256.0), ulp)
        inv = 1.0 / step
        db_s[0:kv, :] = jnp.clip((sc_s[0:kv, :] - lo) * inv, -1.0, 257.0).astype(BF16)

        def digit(_, carry):
            c, bit = carry
            cand = c + bit
            cnt = _count_ge_packed(db_s, kv, cand)
            return jnp.where(cnt >= kf, cand, c), bit * 0.5

        c, _ = lax.fori_loop(0, 8, digit, (jnp.zeros((1, tq), F32), jnp.full((1, tq), 128.0, F32)))
        lo, hi = lo + (c - 1.0) * step, lo + (c + 2.0) * step

    def verify(hi):
        s = sc_s[0:kv, :]
        below = s < hi
        vstar = jnp.max(jnp.where(below, s, neg_inf), axis=0, keepdims=True)
        c_hi = float(kv) - jnp.sum(jnp.where(below, 1.0, 0.0), axis=0, keepdims=True)
        return vstar, c_hi, _count_ge(s, vstar)

    def is_ok(c_hi, c_ge):
        ok = jnp.logical_and(c_hi < kf, c_ge >= kf)
        return ok if few is None else jnp.logical_or(ok, few)

    def unfinished(st):
        _, _, _, _, c_hi, c_ge, n = st
        bad = jnp.max(jnp.where(is_ok(c_hi, c_ge), 0.0, 1.0))
        return jnp.logical_and(bad > 0.5, n < TOPK_MAX_FIX_ROUNDS)

    def fix(st):
        lo, hi, span, vstar, c_hi, c_ge, n = st
        bad = jnp.logical_not(is_ok(c_hi, c_ge))
        hi_true = jnp.logical_and(bad, c_hi >= kf)
        lo = jnp.where(hi_true, hi, lo)
        hi = jnp.where(hi_true, jnp.minimum(hi + span, hi_top),
                       jnp.where(jnp.logical_and(bad, c_ge < kf), vstar, hi))
        span = jnp.where(hi_true, span * 8.0, span)
        mid = 0.5 * lo + 0.5 * hi
        up = _count_ge(sc_s[0:kv, :], mid) >= kf
        lo = jnp.where(jnp.logical_and(bad, up), mid, lo)
        hi = jnp.where(jnp.logical_and(bad, jnp.logical_not(up)), mid, hi)
        vstar, c_hi, c_ge = verify(hi)
        return lo, hi, span, vstar, c_hi, c_ge, n + 1

    vstar, c_hi, c_ge = verify(hi)
    _, _, _, vstar, c_hi, c_ge, _ = lax.while_loop(
        unfinished, fix, (mn, hi, hi - lo, vstar, c_hi, c_ge, jnp.int32(0)))

    need = kf - c_hi
    if few is not None:
        vstar = jnp.where(few, neg_inf, vstar)
        c_ge = jnp.where(few, 0.0, c_ge)
    has_extra_ties = jnp.max(c_ge) > kf

    @pl.when(jnp.logical_not(has_extra_ties))
    def _():
        keep = sc_s[0:kv, :] >= vstar
        if few is not None:
            keep = jnp.logical_and(keep, causal)
        bt_s[0:kv, :] = jnp.where(keep, 0.0, neg_inf)

    @pl.when(has_extra_ties)
    def _():
        ri = lax.broadcasted_iota(jnp.int32, (TIE_CHUNK, TIE_CHUNK), 0)
        ci = lax.broadcasted_iota(jnp.int32, (TIE_CHUNK, TIE_CHUNK), 1)
        tri = jnp.where(ci <= ri, 1.0, 0.0).astype(BF16)
        seen = jnp.zeros((1, tq), F32)
        for c0 in range(0, kv, TIE_CHUNK):
            s = sc_s[c0:c0 + TIE_CHUNK, :]
            eq = jnp.where(s == vstar, 1.0, 0.0)
            rank = _dot(tri, eq.astype(BF16)) + seen
            seen = seen + jnp.sum(eq, axis=0, keepdims=True)
            keep = jnp.logical_or(s > vstar, jnp.logical_and(s == vstar, rank <= need))
            if few is not None:
                keep = jnp.logical_and(keep, causal[c0:c0 + TIE_CHUNK, :])
            bt_s[c0:c0 + TIE_CHUNK, :] = jnp.where(keep, 0.0, neg_inf)


def _dsa_kernel(q_ref, iq_ref, small_ref, tc_ref, ta_ref, tb_ref, o_ref,
                k_s, ik_s, v_s, sc_s, db_s, bt_s, bias_s, *, seq):
    tq = DSA_TQ
    w_scale = (IDX_HEADS ** -0.5) * (HEAD_DIM ** -0.5)
    att_scale = HEAD_DIM ** -0.5
    n_sel = min(IDX_TOPK, seq // 4)
    lane = lax.broadcasted_iota(jnp.int32, (1, LANES), 1)
    lo_half = lane < HEAD_DIM

    tc = tc_ref[...]
    ta = ta_ref[...]
    tb = tb_ref[...]
    k_s[...] = _rope_group(small_ref[:, 0:LANES], tc, ta, tb).astype(BF16)
    ik_s[...] = _rope_group(small_ref[:, LANES:2 * LANES], tc, ta, tb).astype(BF16)
    v_s[...] = small_ref[:, 2 * LANES:3 * LANES].astype(BF16)

    def rope_split(ref, r0, g, scale=None):
        xg = ref[r0:r0 + tq, g * LANES:(g + 1) * LANES].astype(F32)
        xr = _rope_group(xg, tc_ref[r0:r0 + tq, :], ta_ref[r0:r0 + tq, :], tb_ref[r0:r0 + tq, :])
        if scale is not None:
            xr = xr * scale
        return (jnp.where(lo_half, xr, 0.0).astype(BF16), jnp.where(lo_half, 0.0, xr).astype(BF16))

    for blk in range(seq // tq):
        r0 = blk * tq
        kv = r0 + tq
        iw_t = jnp.transpose(small_ref[r0:r0 + tq, 3 * LANES:4 * LANES])
        ik = ik_s[0:kv, :]
        score = jnp.zeros((kv, tq), F32)
        for g in range(IDX_HEADS // 2):
            halves = rope_split(iq_ref, r0, g)
            for par in range(2):
                hh = 2 * g + par
                logits = _dot_nt(ik, halves[par])
                score = score + jnp.maximum(logits, 0.0) * iw_t[hh:hh + 1, :]
        score = score * w_scale
        key_pos = lax.broadcasted_iota(jnp.int32, (kv, tq), 0)
        q_pos = r0 + lax.broadcasted_iota(jnp.int32, (kv, tq), 1)
        causal = key_pos <= q_pos
        sc_s[0:kv, :] = jnp.where(causal, score, -jnp.inf)

        if kv <= n_sel:
            bt_s[0:kv, :] = jnp.where(
                jnp.logical_and(causal, sc_s[0:kv, :] >= -jnp.inf), 0.0, -jnp.inf)
        else:
            few = None
            if r0 < n_sel:
                few = r0 + lax.broadcasted_iota(jnp.int32, (1, tq), 1) < n_sel
            _select_topk(sc_s, db_s, bt_s, kv, float(n_sel), causal, few)
        bias_s[:, 0:kv] = jnp.transpose(bt_s[0:kv, :])

        kk = k_s[0:kv, :]
        vv = v_s[0:kv, :]
        for g in range(ATT_HEADS // 2):
            halves = rope_split(q_ref, r0, g, att_scale)
            outs = []
            for par in range(2):
                s = _dot_nt(halves[par], kk) + bias_s[:, 0:kv]
                m = jnp.max(s, axis=1, keepdims=True)
                p = jnp.exp(s - m)
                l = jnp.sum(p, axis=1, keepdims=True)
                outs.append(_dot(p.astype(BF16), vv) / l)
            o_ref[r0:r0 + tq, g * LANES:(g + 1) * LANES] = (
                jnp.where(lo_half, outs[0], outs[1]).astype(o_ref.dtype))


def _dsa(main, small, tabs, batch, seq):
    t = main.shape[0]
    nh = ATT_HEADS * HEAD_DIM
    const = lambda b: (0, 0)
    kern = functools.partial(_dsa_kernel, seq=seq)
    return pl.pallas_call(
        kern,
        grid=(batch,),
        in_specs=[
            pl.BlockSpec((seq, nh), lambda b: (b, 2)),
            pl.BlockSpec((seq, nh), lambda b: (b, 3)),
            pl.BlockSpec((seq, 4 * LANES), lambda b: (b, 0)),
            pl.BlockSpec((seq, LANES), const),
            pl.BlockSpec((seq, LANES), const),
            pl.BlockSpec((seq, LANES), const),
        ],
        out_specs=pl.BlockSpec((seq, nh), lambda b: (b, 0)),
        out_shape=jax.ShapeDtypeStruct((t, nh), BF16),
        scratch_shapes=[
            pltpu.VMEM((seq, LANES), BF16),
            pltpu.VMEM((seq, LANES), BF16),
            pltpu.VMEM((seq, LANES), BF16),
            pltpu.VMEM((seq, DSA_TQ), F32),
            pltpu.VMEM((seq, DSA_TQ), BF16),
            pltpu.VMEM((seq, DSA_TQ), F32),
            pltpu.VMEM((DSA_TQ, seq), F32),
        ],
        compiler_params=pltpu.CompilerParams(
            dimension_semantics=("arbitrary",), vmem_limit_bytes=VMEM_LIMIT),
    )(main, main, small, *tabs)


def _rope_lane_tables(seq):
    half = ROPE_ROT // 2
    inv = ROPE_THETA ** (-jnp.arange(0, ROPE_ROT, 2, dtype=F32) / ROPE_ROT)
    ang = jnp.arange(seq, dtype=F32)[:, None] * inv[None, :]
    cos, sin = jnp.cos(ang), jnp.sin(ang)
    rest = HEAD_DIM - ROPE_ROT
    c64 = jnp.concatenate([cos, cos, jnp.ones((seq, rest), F32)], axis=1)
    a64 = jnp.concatenate([-sin, jnp.zeros((seq, HEAD_DIM - half), F32)], axis=1)
    b64 = jnp.concatenate([jnp.zeros((seq, half), F32), sin, jnp.zeros((seq, rest), F32)], axis=1)
    return tuple(jnp.tile(x, (1, LANES // HEAD_DIM)) for x in (c64, a64, b64))


def _gla_stages(q_ref, k_ref, v_ref, r_ref, glr_ref, gw_ref, gb_ref, og_ref, o_ref,
                state, bc_s, qt_s, kt_s, ksc_s, fresh):
    tb, c, sb = GLA_TB, GLA_CHUNK, GLA_SUB
    tok = lambda x: x[0:BF16_ROWS, 0:LANES].astype(F32)

    ri = lax.broadcasted_iota(jnp.int32, (sb, sb), 0)
    ci = lax.broadcasted_iota(jnp.int32, (sb, sb), 1)
    shift = c.bit_length() - 1
    tri = jnp.logical_and(ci <= ri, lax.shift_right_logical(ri, shift) == lax.shift_right_logical(ci, shift))
    tri_b = jnp.where(tri, 1.0, 0.0).astype(BF16)
    gw_h, gw_m, gw_l = _split3(gw_ref[...])
    gb = gb_ref[...]
    og = og_ref[...]
    cmask = (lax.broadcasted_iota(jnp.int32, (c, c), 1) <= lax.broadcasted_iota(jnp.int32, (c, c), 0))

    def decays(s0):
        rows = slice(s0, s0 + sb)
        gl_h, gl_m, gl_l = _split3(glr_ref[rows, :])
        pre = (_dot(gl_h, gw_h) + (_dot(gl_h, gw_m) + _dot(gl_m, gw_h))
               + (_dot(gl_h, gw_l) + _dot(gl_m, gw_m) + _dot(gl_l, gw_h))) + gb
        log_a = (jnp.minimum(pre, 0.0) - jnp.log(1.0 + jnp.exp(-jnp.abs(pre)))) / GLA_TAU
        la_h, la_m, la_l = _split3(log_a)
        bcum = _dot(tri_b, la_h) + _dot(tri_b, la_m) + _dot(tri_b, la_l)
        bc_s[rows, :] = bcum
        qt = q_ref[rows, :].astype(F32) * (GLA_DK ** -0.5) * jnp.exp(bcum)
        qt_s[rows, :] = qt.astype(BF16)
        kt_s[rows, :] = (k_ref[rows, :].astype(F32) * jnp.exp(-bcum)).astype(BF16)
        return tok(qt)

    def scaled_keys(t0):
        rows = slice(t0, t0 + c)
        b_last = bc_s[t0 + c - 1:t0 + c, :]
        ksc = k_ref[rows, :].astype(F32) * jnp.exp(b_last - bc_s[rows, :])
        ksc_s[rows, :] = ksc.astype(BF16)
        return tok(ksc)

    carried = {}

    def head_chunk(hd, t0):
        ks = slice(hd * GLA_DK, (hd + 1) * GLA_DK)
        vs = slice(hd * GLA_DV, (hd + 1) * GLA_DV)
        rows = slice(t0, t0 + c)
        st = jnp.where(fresh, 0.0, state[hd]) if t0 == 0 else carried[hd]
        qh = qt_s[rows, ks]
        vh = v_ref[rows, vs]
        att = jnp.where(cmask, _dot_nt(qh, kt_s[rows, ks]), 0.0)
        o = _dot(att.astype(BF16), vh) + _dot_nt(qh, st.astype(BF16))
        st = st * jnp.exp(bc_s[t0 + c - 1:t0 + c, ks]) + _dot_tn(vh, ksc_s[rows, ks])
        o = o * lax.rsqrt(jnp.mean(o * o, axis=-1, keepdims=True) + NORM_EPS) * og[:, vs]
        rr = r_ref[rows, vs].astype(F32)
        o = o * (rr * _sigmoid(rr))
        o_ref[rows, vs] = o.astype(o_ref.dtype)
        if t0 + c == tb:
            state[hd] = st
        else:
            carried[hd] = st
        return tok(o)

    steps = [functools.partial(decays, s0) for s0 in range(0, tb, sb)]
    steps += [functools.partial(scaled_keys, t0) for t0 in range(0, tb, c)]
    steps += [functools.partial(head_chunk, hd, t0) for t0 in range(0, tb, c) for hd in range(GLA_HEADS)]
    return steps


def _out_ffn_kernel(*refs, side_kind, tiles_per_seq, final):
    step = pl.program_id(0)
    seq_start = lax.rem(step, tiles_per_seq) == 0
    h_ref = refs[0]
    if side_kind == "conv":
        (yb_ref, ua_ref, cw_ref, cb_ref, lg_ref, lb_ref, wo_ref, g_ref, wg_ref, wu_ref, wd_ref, fg_ref,
         o_ref, a_scr, y_s, y_next, hbuf, ysh) = refs[1:]
        carry = hbuf
        y_refs = [y_s, yb_ref]
    else:
        (q_ref, k_ref, v_ref, r_ref, glr_ref, gw_ref, gb_ref, og_ref, wo_ref, g_ref, wg_ref, wu_ref,
         wd_ref, fg_ref, o_ref, a_scr, y_s, y_next, state, bc_s, qt_s, kt_s, ksc_s) = refs[1:]
        carry = state
        y_refs = [y_s]

    @pl.when(step == 0)
    def _():
        carry[...] = jnp.zeros(carry.shape, F32)
        y_s[...] = jnp.zeros(y_s.shape, BF16)

    if side_kind == "conv":
        cb, lg, lb = cb_ref[...], lg_ref[...], lb_ref[...]
        side = []
        for r in range(0, FFN_TM, CONV_TS):
            fresh = seq_start if r == 0 else False
            side += _conv_stages(ua_ref, r, cw_ref, cb, lg, lb, y_next, r, hbuf, ysh, fresh)
    else:
        side = _gla_stages(q_ref, k_ref, v_ref, r_ref, glr_ref, gw_ref, gb_ref, og_ref, y_next,
                           state, bc_s, qt_s, kt_s, ksc_s, seq_start)
    n_chunks = D_FF // FFN_FC

    def run_side(slot):
        lo, hi = (len(side) * slot) // n_chunks, (len(side) * (slot + 1)) // n_chunks
        toks = [t for t in (fn() for fn in side[lo:hi]) if t is not None]
        if not toks:
            return None
        bits = pltpu.bitcast(functools.reduce(lambda a, b: a + b, toks), jnp.uint32)
        zero = lax.shift_right_logical(lax.shift_right_logical(bits, jnp.uint32(16)), jnp.uint32(16))
        return pltpu.bitcast(zero, F32).astype(BF16)

    acc = h_ref[...]
    k0 = 0
    for y_ref in y_refs:
        kw = y_ref.shape[1]
        acc = acc + _dot(y_ref[...], wo_ref[k0:k0 + kw, :])
        k0 += kw
    o_ref[...] = acc
    hn = _rms(acc, g_ref[...]).astype(BF16)
    for ci in range(n_chunks):
        f0 = ci * FFN_FC
        gt = _dot(hn, wg_ref[:, f0:f0 + FFN_FC])
        up = _dot(hn, wu_ref[:, f0:f0 + FFN_FC])
        a_scr[:, f0:f0 + FFN_FC] = (gt * _sigmoid(gt) * up).astype(BF16)
        zero = run_side(ci)
        if zero is not None:
            a_scr[0:BF16_ROWS, f0:f0 + LANES] = a_scr[0:BF16_ROWS, f0:f0 + LANES] + zero
    out = o_ref[...] + _dot(a_scr[...], wd_ref[...])
    if final:
        out = _rms(out, fg_ref[...])
    o_ref[...] = out
    y_s[...] = y_next[...]


def _out_ffn(h, side_kind, side_args, seq, w_out, g, w_gate, w_up, w_down, final_g, final):
    t = h.shape[0]
    n = t // FFN_TM
    const = lambda i: (0, 0)
    resident = lambda shape: pl.BlockSpec(shape, const, pipeline_mode=pl.Buffered(1))
    row = lambda i: (jnp.maximum(i - 1, 0), 0)
    ahead = lambda col: (lambda i: (jnp.minimum(i, n - 1), col))
    if side_kind == "conv":
        yb, main, conv_w, conv_b, ln_g, ln_b = side_args
        cw = jnp.concatenate([conv_w.reshape(CONV_WIDTH, CONV_CH), jnp.zeros((1, CONV_CH), F32)], axis=0)
        vec = lambda v: v.reshape(1, CONV_CH)
        y_width = CONV_CH
        side_specs = [pl.BlockSpec((FFN_TM, yb.shape[1]), row),
                      pl.BlockSpec((FFN_TM, 2 * CONV_CH), ahead(0)),
                      pl.BlockSpec((CONV_WIDTH + 1, CONV_CH), const),
                      pl.BlockSpec((1, CONV_CH), const),
                      pl.BlockSpec((1, CONV_CH), const),
                      pl.BlockSpec((1, CONV_CH), const)]
        side_ops = [yb, main, cw, vec(conv_b), vec(ln_g), vec(ln_b)]
        side_scratch = [pltpu.VMEM((CONV_HALO + CONV_TS, CONV_CH), F32),
                        pltpu.VMEM((SUBLANES - 1, CONV_TS + CONV_HALO - SUBLANES, CONV_CH), F32)]
    else:
        assert GLA_TB == FFN_TM
        main, glr, gate_w, gate_b, onorm_g = side_args
        nk = GLA_HEADS * GLA_DK
        nv = GLA_HEADS * GLA_DV
        gw = jnp.concatenate([gate_w, jnp.zeros((LANES - GLA_GATE_RANK, nk), F32)], axis=0)
        y_width = nv
        side_specs = [pl.BlockSpec((GLA_TB, nk), ahead(0)),
                      pl.BlockSpec((GLA_TB, nk), ahead(1)),
                      pl.BlockSpec((GLA_TB, nv), ahead(1)),
                      pl.BlockSpec((GLA_TB, nv), ahead(2)),
                      pl.BlockSpec((GLA_TB, LANES), ahead(0)),
                      pl.BlockSpec((LANES, nk), const),
                      pl.BlockSpec((1, nk), const),
                      pl.BlockSpec((1, nv), const)]
        side_ops = [main, main, main, main, glr, gw, gate_b.reshape(1, nk), onorm_g.reshape(1, nv)]
        side_scratch = [pltpu.VMEM((GLA_HEADS, GLA_DV, GLA_DK), F32),
                        pltpu.VMEM((GLA_TB, nk), F32),
                        pltpu.VMEM((GLA_TB, nk), BF16),
                        pltpu.VMEM((GLA_TB, nk), BF16),
                        pltpu.VMEM((GLA_TB, nk), BF16)]
    kern = functools.partial(_out_ffn_kernel, side_kind=side_kind, tiles_per_seq=seq // FFN_TM,
                             final=final)
    return pl.pallas_call(
        kern,
        grid=(n + 1,),
        in_specs=(
            [pl.BlockSpec((FFN_TM, D_MODEL), row)]
            + side_specs
            + [resident((w_out.shape[0], D_MODEL)),
               pl.BlockSpec((1, D_MODEL), const),
               resident((D_MODEL, D_FF)),
               resident((D_MODEL, D_FF)),
               resident((D_FF, D_MODEL)),
               pl.BlockSpec((1, D_MODEL), const)]),
        out_specs=pl.BlockSpec((FFN_TM, D_MODEL), row),
        out_shape=jax.ShapeDtypeStruct((t, D_MODEL), F32),
        scratch_shapes=[pltpu.VMEM((FFN_TM, D_FF), BF16),
                        pltpu.VMEM((FFN_TM, y_width), BF16),
                        pltpu.VMEM((FFN_TM, y_width), BF16)] + side_scratch,
        compiler_params=pltpu.CompilerParams(
            dimension_semantics=("arbitrary",), vmem_limit_bytes=VMEM_LIMIT),
    )(h, *side_ops, w_out.astype(BF16), g.reshape(1, D_MODEL), w_gate.astype(BF16),
      w_up.astype(BF16), w_down.astype(BF16), final_g.reshape(1, D_MODEL))


def kernel(x, norm_mix_g, ab_w_in, ab_conv_w, ab_conv_b, ab_ln_g, ab_ln_b, ab_w_out,
           c_w_in, c_gate_w, c_gate_b, c_onorm_g, c_w_out,
           norm_ffn_g, ffn_w_gate, ffn_w_up, ffn_w_down, final_norm_g):
    batch, seq, _ = x.shape
    t = batch * seq
    h = x.reshape(t, D_MODEL)
    nh = ATT_HEADS * HEAD_DIM

    w = ab_w_in[0]
    o_q = 2 * CONV_CH
    o_k = o_q + nh
    o_v = o_k + HEAD_DIM
    o_iq = o_v + HEAD_DIM
    o_ik = o_iq + IDX_HEADS * HEAD_DIM
    o_iw = o_ik + HEAD_DIM
    w_k, w_v, w_ik = w[:, o_k:o_v], w[:, o_v:o_iq], w[:, o_ik:o_iw]
    w_iw = jnp.concatenate([w[:, o_iw:], jnp.zeros((D_MODEL, LANES - IDX_HEADS), F32)], axis=1)
    w_all = jnp.concatenate(
        [w[:, :o_k], w[:, o_iq:o_ik], w_k, w_k, w_ik, w_ik, w_v, w_v, w_iw], axis=1).astype(BF16)
    n_main = 2 * CONV_CH + 2 * nh
    main, small = _norm_proj(h, norm_mix_g[0], w_all,
                             ((0, n_main), (n_main, n_main + 4 * LANES)), (BF16, F32))
    yb = _dsa(main, small, _rope_lane_tables(seq), batch, seq)
    h = _out_ffn(h, "conv", (yb, main, ab_conv_w[0], ab_conv_b[0], ab_ln_g[0], ab_ln_b[0]), seq,
                 ab_w_out[0], norm_ffn_g[0], ffn_w_gate[0], ffn_w_up[0], ffn_w_down[0],
                 final_norm_g, False)

    w = c_w_in[0]
    n_main = 2 * GLA_HEADS * GLA_DK + 2 * GLA_HEADS * GLA_DV
    w_all = jnp.concatenate(
        [w, jnp.zeros((D_MODEL, LANES - GLA_GATE_RANK), F32)], axis=1).astype(BF16)
    main, glr = _norm_proj(h, norm_mix_g[1], w_all,
                           ((0, n_main), (n_main, n_main + LANES)), (BF16, F32))
    h = _out_ffn(h, "gla", (main, glr, c_gate_w[0], c_gate_b[0], c_onorm_g[0]), seq,
                 c_w_out[0], norm_ffn_g[1], ffn_w_gate[1], ffn_w_up[1], ffn_w_down[1],
                 final_norm_g, True)
    return h.reshape(batch, seq, D_MODEL)
```

```python
import functools

import jax
import jax.numpy as jnp
from jax import lax
from jax.experimental import pallas as pl
from jax.experimental.pallas import tpu as pltpu

F32 = jnp.float32
BF16 = jnp.bfloat16

D_MODEL = 1024
CONV_CH = 512
CONV_WIDTH = 31
ATT_HEADS = 8
HEAD_DIM = 64
IDX_HEADS = 8
IDX_TOPK = 256
GLA_HEADS = 4
GLA_DK = 128
GLA_DV = 256
GLA_GATE_RANK = 16
GLA_TAU = 16.0
GLA_CHUNK = 64
ROPE_THETA = 500000.0
ROPE_ROT = HEAD_DIM // 4
D_FF = 2816
NORM_EPS = 1e-6
LOG2_E = 1.4426950408889634

LANES = 128
SUBLANES = 8
VMEM_LIMIT = 56 * 1024 * 1024

PROJ_TM = 512
FFN_TM = 512
FFN_FC = 256
CONV_TS = 256
CONV_RC = 32
CONV_HALO = 32
DSA_TQ = 256
GLA_TB = 512
GLA_SUB = 256
BF16_ROWS = 16
F32_EPS = 2.0 ** -23
TOPK_LEVELS = 4
TOPK_MAX_FIX_ROUNDS = 400
TIE_CHUNK = 256
COUNT_ACCS = 4


def _sigmoid(x):
    return 1.0 / (1.0 + jnp.exp(-x))


def _rms(x, g):
    ms = jnp.mean(x * x, axis=-1, keepdims=True)
    return x * lax.rsqrt(ms + NORM_EPS) * g


def _dot(a, b):
    return jnp.dot(a, b, preferred_element_type=F32)


def _dot_nt(a, b):
    return lax.dot_general(a, b, (((1,), (1,)), ((), ())), preferred_element_type=F32)


def _dot_tn(a, b):
    return lax.dot_general(a, b, (((0,), (0,)), ((), ())), preferred_element_type=F32)


def _split3(x):
    hi = x.astype(BF16)
    r1 = x - hi.astype(F32)
    mid = r1.astype(BF16)
    lo = (r1 - mid.astype(F32)).astype(BF16)
    return hi, mid, lo


def _norm_proj_kernel(x_ref, g_ref, w_ref, *out_refs, groups, nchunk):
    xn = _rms(x_ref[...], g_ref[...]).astype(BF16)
    for (c0, c1), o_ref in zip(groups, out_refs):
        for n0 in range(c0, c1, nchunk):
            n1 = min(n0 + nchunk, c1)
            o_ref[:, n0 - c0:n1 - c0] = _dot(xn, w_ref[:, n0:n1]).astype(o_ref.dtype)


def _norm_proj(h, g, w, groups, dtypes):
    t = h.shape[0]
    n = w.shape[1]
    kern = functools.partial(_norm_proj_kernel, groups=groups, nchunk=512)
    return pl.pallas_call(
        kern,
        grid=(t // PROJ_TM,),
        in_specs=[
            pl.BlockSpec((PROJ_TM, D_MODEL), lambda i: (i, 0)),
            pl.BlockSpec((1, D_MODEL), lambda i: (0, 0)),
            pl.BlockSpec((D_MODEL, n), lambda i: (0, 0)),
        ],
        out_specs=[pl.BlockSpec((PROJ_TM, c1 - c0), lambda i: (i, 0)) for c0, c1 in groups],
        out_shape=[jax.ShapeDtypeStruct((t, c1 - c0), dt) for (c0, c1), dt in zip(groups, dtypes)],
        compiler_params=pltpu.CompilerParams(
            dimension_semantics=("arbitrary",), vmem_limit_bytes=VMEM_LIMIT),
    )(h, g.reshape(1, D_MODEL), w)


def _conv_stages(ua_ref, r_in, w_ref, cb, lg, lb, o_ref, r_out, hbuf, ysh, fresh):
    ts = CONV_TS
    base = CONV_HALO - (CONV_WIDTH - 1)

    def prepare():
        tail = hbuf[ts:ts + CONV_HALO, :]
        hbuf[0:CONV_HALO, :] = jnp.where(fresh, 0.0, tail)
        a = ua_ref[r_in:r_in + ts, 0:CONV_CH].astype(F32)
        gate = ua_ref[r_in:r_in + ts, CONV_CH:2 * CONV_CH].astype(F32)
        hbuf[CONV_HALO:CONV_HALO + ts, :] = a * _sigmoid(gate)
        ylen = ysh.shape[1]
        for b in range(1, SUBLANES):
            ysh[b - 1] = hbuf[b:b + ylen, :]
        return None

    def rows(r):
        acc = jnp.zeros((CONV_RC, CONV_CH), F32)
        for s in range(base, base + CONV_WIDTH):
            a8, b = divmod(s, SUBLANES)
            r8 = r + a8 * SUBLANES
            tap = hbuf[r8:r8 + CONV_RC, :] if b == 0 else ysh[b - 1, r8:r8 + CONV_RC, :]
            acc = acc + tap * w_ref[s - base:s - base + 1, :]
        acc = acc + cb
        mu = jnp.mean(acc, axis=-1, keepdims=True)
        d = acc - mu
        var = jnp.mean(d * d, axis=-1, keepdims=True)
        y = d * lax.rsqrt(var + NORM_EPS) * lg + lb
        o_ref[r_out + r:r_out + r + CONV_RC, :] = (y * _sigmoid(y)).astype(o_ref.dtype)
        return y[0:BF16_ROWS, 0:LANES]

    return [prepare] + [functools.partial(rows, r) for r in range(0, ts, CONV_RC)]


def _rope_group(xg, c, a, bm):
    return xg * c + pltpu.roll(xg, LANES - ROPE_ROT // 2, 1) * a + pltpu.roll(xg, ROPE_ROT // 2, 1) * bm


def _count_ge(x, c):
    return jnp.sum(jnp.where(x >= c, 1.0, 0.0), axis=0, keepdims=True)


def _count_ge_packed(db_ref, kv, cand):
    tq = db_ref.shape[1]
    cand_b = jnp.broadcast_to(cand, (BF16_ROWS, tq)).astype(BF16)
    one = jnp.ones((BF16_ROWS, tq), BF16)
    zero = jnp.zeros((BF16_ROWS, tq), BF16)
    accs = [None] * COUNT_ACCS
    for i, r in enumerate(range(0, kv, BF16_ROWS)):
        part = jnp.where(db_ref[r:r + BF16_ROWS, :] >= cand_b, one, zero)
        j = i % COUNT_ACCS
        accs[j] = part if accs[j] is None else accs[j] + part
    total = functools.reduce(lambda a, b: a + b, [a for a in accs if a is not None])
    return jnp.sum(total.astype(F32), axis=0, keepdims=True)


def _select_topk(sc_s, db_s, bt_s, kv, kf, causal, few):
    tq = sc_s.shape[1]
    neg_inf = -jnp.inf
    s_all = sc_s[0:kv, :]
    mx = jnp.max(s_all, axis=0, keepdims=True)
    mn = jnp.min(jnp.where(causal, s_all, jnp.inf), axis=0, keepdims=True)
    hi_top = mx + jnp.maximum((mx - mn) * (1.0 / 64.0), jnp.abs(mx) * (2.0 ** -20) + 1e-30)

    lo, hi = mn, hi_top
    for _ in range(TOPK_LEVELS):
        ulp = jnp.maximum(jnp.maximum(jnp.abs(lo), jnp.abs(hi)) * F32_EPS, 1e-30)
        step = jnp.maximum((hi - lo) * (1.0 / 256.0), ulp)
        inv = 1.0 / step
        db_s[0:kv, :] = jnp.clip((sc_s[0:kv, :] - lo) * inv, -1.0, 257.0).astype(BF16)

        def digit(_, carry):
            c, bit = carry
            cand = c + bit
            cnt = _count_ge_packed(db_s, kv, cand)
            return jnp.where(cnt >= kf, cand, c), bit * 0.5

        c, _ = lax.fori_loop(0, 8, digit, (jnp.zeros((1, tq), F32), jnp.full((1, tq), 128.0, F32)))
        lo, hi = lo + (c - 1.0) * step, lo + (c + 2.0) * step

    def verify(hi):
        s = sc_s[0:kv, :]
        below = s < hi
        vstar = jnp.max(jnp.where(below, s, neg_inf), axis=0, keepdims=True)
        c_hi = float(kv) - jnp.sum(jnp.where(below, 1.0, 0.0), axis=0, keepdims=True)
        return vstar, c_hi, _count_ge(s, vstar)

    def is_ok(c_hi, c_ge):
        ok = jnp.logical_and(c_hi < kf, c_ge >= kf)
        return ok if few is None else jnp.logical_or(ok, few)

    def unfinished(st):
        _, _, _, _, c_hi, c_ge, n = st
        bad = jnp.max(jnp.where(is_ok(c_hi, c_ge), 0.0, 1.0))
        return jnp.logical_and(bad > 0.5, n < TOPK_MAX_FIX_ROUNDS)

    def fix(st):
        lo, hi, span, vstar, c_hi, c_ge, n = st
        bad = jnp.logical_not(is_ok(c_hi, c_ge))
        hi_true = jnp.logical_and(bad, c_hi >= kf)
        lo = jnp.where(hi_true, hi, lo)
        hi = jnp.where(hi_true, jnp.minimum(hi + span, hi_top),
                       jnp.where(jnp.logical_and(bad, c_ge < kf), vstar, hi))
        span = jnp.where(hi_true, span * 8.0, span)
        mid = 0.5 * lo + 0.5 * hi
        up = _count_ge(sc_s[0:kv, :], mid) >= kf
        lo = jnp.where(jnp.logical_and(bad, up), mid, lo)
        hi = jnp.where(jnp.logical_and(bad, jnp.logical_not(up)), mid, hi)
        vstar, c_hi, c_ge = verify(hi)
        return lo, hi, span, vstar, c_hi, c_ge, n + 1

    vstar, c_hi, c_ge = verify(hi)
    _, _, _, vstar, c_hi, c_ge, _ = lax.while_loop(
        unfinished, fix, (mn, hi, hi - lo, vstar, c_hi, c_ge, jnp.int32(0)))

    need = kf - c_hi
    if few is not None:
        vstar = jnp.where(few, neg_inf, vstar)
        c_ge = jnp.where(few, 0.0, c_ge)
    has_extra_ties = jnp.max(c_ge) > kf

    @pl.when(jnp.logical_not(has_extra_ties))
    def _():
        keep = sc_s[0:kv, :] >= vstar
        if few is not None:
            keep = jnp.logical_and(keep, causal)
        bt_s[0:kv, :] = jnp.where(keep, 0.0, neg_inf)

    @pl.when(has_extra_ties)
    def _():
        ri = lax.broadcasted_iota(jnp.int32, (TIE_CHUNK, TIE_CHUNK), 0)
        ci = lax.broadcasted_iota(jnp.int32, (TIE_CHUNK, TIE_CHUNK), 1)
        tri = jnp.where(ci <= ri, 1.0, 0.0).astype(BF16)
        seen = jnp.zeros((1, tq), F32)
        for c0 in range(0, kv, TIE_CHUNK):
            s = sc_s[c0:c0 + TIE_CHUNK, :]
            eq = jnp.where(s == vstar, 1.0, 0.0)
            rank = _dot(tri, eq.astype(BF16)) + seen
            seen = seen + jnp.sum(eq, axis=0, keepdims=True)
            keep = jnp.logical_or(s > vstar, jnp.logical_and(s == vstar, rank <= need))
            if few is not None:
                keep = jnp.logical_and(keep, causal[c0:c0 + TIE_CHUNK, :])
            bt_s[c0:c0 + TIE_CHUNK, :] = jnp.where(keep, 0.0, neg_inf)


def _dsa_kernel(q_ref, iq_ref, small_ref, tc_ref, ta_ref, tb_ref, o_ref,
                k_s, ik_s, v_s, sc_s, db_s, bt_s, bias_s, *, seq):
    tq = DSA_TQ
    w_scale = (IDX_HEADS ** -0.5) * (HEAD_DIM ** -0.5)
    att_scale = HEAD_DIM ** -0.5
    n_sel = min(IDX_TOPK, seq // 4)
    lane = lax.broadcasted_iota(jnp.int32, (1, LANES), 1)
    lo_half = lane < HEAD_DIM

    tc = tc_ref[...]
    ta = ta_ref[...]
    tb = tb_ref[...]
    k_s[...] = _rope_group(small_ref[:, 0:LANES], tc, ta, tb).astype(BF16)
    ik_s[...] = _rope_group(small_ref[:, LANES:2 * LANES], tc, ta, tb).astype(BF16)
    vv = small_ref[:, 2 * LANES:3 * LANES]
    v_s[0] = jnp.where(lo_half, vv, 1.0).astype(BF16)
    v_s[1] = jnp.where(lo_half, 1.0, vv).astype(BF16)

    def rope_split(ref, r0, g, scale=None):
        xg = ref[r0:r0 + tq, g * LANES:(g + 1) * LANES].astype(F32)
        xr = _rope_group(xg, tc_ref[r0:r0 + tq, :], ta_ref[r0:r0 + tq, :], tb_ref[r0:r0 + tq, :])
        if scale is not None:
            xr = xr * scale
        return (jnp.where(lo_half, xr, 0.0).astype(BF16), jnp.where(lo_half, 0.0, xr).astype(BF16))

    for blk in range(seq // tq):
        r0 = blk * tq
        kv = r0 + tq
        iw_t = jnp.transpose(small_ref[r0:r0 + tq, 3 * LANES:4 * LANES])
        ik = ik_s[0:kv, :]
        score = jnp.zeros((kv, tq), F32)
        for g in range(IDX_HEADS // 2):
            halves = rope_split(iq_ref, r0, g)
            for par in range(2):
                hh = 2 * g + par
                logits = _dot_nt(ik, halves[par])
                score = score + jnp.maximum(logits, 0.0) * iw_t[hh:hh + 1, :]
        score = score * w_scale
        key_pos = lax.broadcasted_iota(jnp.int32, (kv, tq), 0)
        q_pos = r0 + lax.broadcasted_iota(jnp.int32, (kv, tq), 1)
        causal = key_pos <= q_pos
        sc_s[0:kv, :] = jnp.where(causal, score, -jnp.inf)

        if kv <= n_sel:
            bt_s[0:kv, :] = jnp.where(
                jnp.logical_and(causal, sc_s[0:kv, :] >= -jnp.inf), 0.0, -jnp.inf)
        else:
            few = None
            if r0 < n_sel:
                few = r0 + lax.broadcasted_iota(jnp.int32, (1, tq), 1) < n_sel
            _select_topk(sc_s, db_s, bt_s, kv, float(n_sel), causal, few)
        bias_s[:, 0:kv] = jnp.transpose(bt_s[0:kv, :])

        kk = k_s[0:kv, :]
        for g in range(ATT_HEADS // 2):
            halves = rope_split(q_ref, r0, g, att_scale * LOG2_E)
            outs = []
            for par in range(2):
                s = _dot_nt(halves[par], kk) + bias_s[:, 0:kv]
                m = jnp.max(s, axis=1, keepdims=True)
                p = jnp.exp2(s - m)
                o = _dot(p.astype(BF16), v_s[par, 0:kv, :])
                outs.append(o / pltpu.roll(o, HEAD_DIM, 1))
            o_ref[r0:r0 + tq, g * LANES:(g + 1) * LANES] = (
                jnp.where(lo_half, outs[0], outs[1]).astype(o_ref.dtype))


def _dsa(main, small, tabs, batch, seq):
    t = main.shape[0]
    nh = ATT_HEADS * HEAD_DIM
    const = lambda b: (0, 0)
    kern = functools.partial(_dsa_kernel, seq=seq)
    return pl.pallas_call(
        kern,
        grid=(batch,),
        in_specs=[
            pl.BlockSpec((seq, nh), lambda b: (b, 2)),
            pl.BlockSpec((seq, nh), lambda b: (b, 3)),
            pl.BlockSpec((seq, 4 * LANES), lambda b: (b, 0)),
            pl.BlockSpec((seq, LANES), const),
            pl.BlockSpec((seq, LANES), const),
            pl.BlockSpec((seq, LANES), const),
        ],
        out_specs=pl.BlockSpec((seq, nh), lambda b: (b, 0)),
        out_shape=jax.ShapeDtypeStruct((t, nh), BF16),
        scratch_shapes=[
            pltpu.VMEM((seq, LANES), BF16),
            pltpu.VMEM((seq, LANES), BF16),
            pltpu.VMEM((2, seq, LANES), BF16),
            pltpu.VMEM((seq, DSA_TQ), F32),
            pltpu.VMEM((seq, DSA_TQ), BF16),
            pltpu.VMEM((seq, DSA_TQ), F32),
            pltpu.VMEM((DSA_TQ, seq), F32),
        ],
        compiler_params=pltpu.CompilerParams(
            dimension_semantics=("arbitrary",), vmem_limit_bytes=VMEM_LIMIT),
    )(main, main, small, *tabs)


def _rope_lane_tables(seq):
    half = ROPE_ROT // 2
    inv = ROPE_THETA ** (-jnp.arange(0, ROPE_ROT, 2, dtype=F32) / ROPE_ROT)
    ang = jnp.arange(seq, dtype=F32)[:, None] * inv[None, :]
    cos, sin = jnp.cos(ang), jnp.sin(ang)
    rest = HEAD_DIM - ROPE_ROT
    c64 = jnp.concatenate([cos, cos, jnp.ones((seq, rest), F32)], axis=1)
    a64 = jnp.concatenate([-sin, jnp.zeros((seq, HEAD_DIM - half), F32)], axis=1)
    b64 = jnp.concatenate([jnp.zeros((seq, half), F32), sin, jnp.zeros((seq, rest), F32)], axis=1)
    return tuple(jnp.tile(x, (1, LANES // HEAD_DIM)) for x in (c64, a64, b64))


def _gla_stages(q_ref, k_ref, v_ref, r_ref, glr_ref, gw_ref, gb_ref, og_ref, o_ref,
                state, bc_s, qt_s, kt_s, ksc_s, fresh):
    tb, c, sb = GLA_TB, GLA_CHUNK, GLA_SUB
    tok = lambda x: x[0:BF16_ROWS, 0:LANES].astype(F32)

    ri = lax.broadcasted_iota(jnp.int32, (sb, sb), 0)
    ci = lax.broadcasted_iota(jnp.int32, (sb, sb), 1)
    shift = c.bit_length() - 1
    tri = jnp.logical_and(ci <= ri, lax.shift_right_logical(ri, shift) == lax.shift_right_logical(ci, shift))
    tri_b = jnp.where(tri, 1.0, 0.0).astype(BF16)
    gw_h, gw_m, gw_l = _split3(gw_ref[...])
    gb = gb_ref[...]
    og = og_ref[...]
    cmask = (lax.broadcasted_iota(jnp.int32, (c, c), 1) <= lax.broadcasted_iota(jnp.int32, (c, c), 0))

    def decays(s0):
        rows = slice(s0, s0 + sb)
        gl_h, gl_m, gl_l = _split3(glr_ref[rows, :])
        pre = (_dot(gl_h, gw_h) + (_dot(gl_h, gw_m) + _dot(gl_m, gw_h))
               + (_dot(gl_h, gw_l) + _dot(gl_m, gw_m) + _dot(gl_l, gw_h))) + gb
        log_a = (jnp.minimum(pre, 0.0) - jnp.log(1.0 + jnp.exp(-jnp.abs(pre)))) / GLA_TAU
        la_h, la_m, la_l = _split3(log_a)
        bcum = _dot(tri_b, la_h) + _dot(tri_b, la_m) + _dot(tri_b, la_l)
        bc_s[rows, :] = bcum
        qt = q_ref[rows, :].astype(F32) * (GLA_DK ** -0.5) * jnp.exp(bcum)
        qt_s[rows, :] = qt.astype(BF16)
        kt_s[rows, :] = (k_ref[rows, :].astype(F32) * jnp.exp(-bcum)).astype(BF16)
        return tok(qt)

    def scaled_keys(t0):
        rows = slice(t0, t0 + c)
        b_last = bc_s[t0 + c - 1:t0 + c, :]
        ksc = k_ref[rows, :].astype(F32) * jnp.exp(b_last - bc_s[rows, :])
        ksc_s[rows, :] = ksc.astype(BF16)
        return tok(ksc)

    carried = {}

    def head_chunk(hd, t0):
        ks = slice(hd * GLA_DK, (hd + 1) * GLA_DK)
        vs = slice(hd * GLA_DV, (hd + 1) * GLA_DV)
        rows = slice(t0, t0 + c)
        st = jnp.where(fresh, 0.0, state[hd]) if t0 == 0 else carried[hd]
        qh = qt_s[rows, ks]
        vh = v_ref[rows, vs]
        att = jnp.where(cmask, _dot_nt(qh, kt_s[rows, ks]), 0.0)
        o = _dot(att.astype(BF16), vh) + _dot_nt(qh, st.astype(BF16))
        st = st * jnp.exp(bc_s[t0 + c - 1:t0 + c, ks]) + _dot_tn(vh, ksc_s[rows, ks])
        o = o * lax.rsqrt(jnp.mean(o * o, axis=-1, keepdims=True) + NORM_EPS) * og[:, vs]
        rr = r_ref[rows, vs].astype(F32)
        o = o * (rr * _sigmoid(rr))
        o_ref[rows, vs] = o.astype(o_ref.dtype)
        if t0 + c == tb:
            state[hd] = st
        else:
            carried[hd] = st
        return tok(o)

    steps = [functools.partial(decays, s0) for s0 in range(0, tb, sb)]
    steps += [functools.partial(scaled_keys, t0) for t0 in range(0, tb, c)]
    steps += [functools.partial(head_chunk, hd, t0) for t0 in range(0, tb, c) for hd in range(GLA_HEADS)]
    return steps


def _out_ffn_kernel(*refs, side_kind, tiles_per_seq, final):
    step = pl.program_id(0)
    seq_start = lax.rem(step, tiles_per_seq) == 0
    h_ref = refs[0]
    if side_kind == "conv":
        (yb_ref, ua_ref, cw_ref, cb_ref, lg_ref, lb_ref, wo_ref, g_ref, wg_ref, wu_ref, wd_ref, fg_ref,
         o_ref, a_scr, y_s, y_next, hbuf, ysh) = refs[1:]
        carry = hbuf
        y_refs = [y_s, yb_ref]
    else:
        (q_ref, k_ref, v_ref, r_ref, glr_ref, gw_ref, gb_ref, og_ref, wo_ref, g_ref, wg_ref, wu_ref,
         wd_ref, fg_ref, o_ref, a_scr, y_s, y_next, state, bc_s, qt_s, kt_s, ksc_s) = refs[1:]
        carry = state
        y_refs = [y_s]

    @pl.when(step == 0)
    def _():
        carry[...] = jnp.zeros(carry.shape, F32)
        y_s[...] = jnp.zeros(y_s.shape, BF16)

    if side_kind == "conv":
        cb, lg, lb = cb_ref[...], lg_ref[...], lb_ref[...]
        side = []
        for r in range(0, FFN_TM, CONV_TS):
            fresh = seq_start if r == 0 else False
            side += _conv_stages(ua_ref, r, cw_ref, cb, lg, lb, y_next, r, hbuf, ysh, fresh)
    else:
        side = _gla_stages(q_ref, k_ref, v_ref, r_ref, glr_ref, gw_ref, gb_ref, og_ref, y_next,
                           state, bc_s, qt_s, kt_s, ksc_s, seq_start)
    n_chunks = D_FF // FFN_FC

    def run_side(slot):
        lo, hi = (len(side) * slot) // n_chunks, (len(side) * (slot + 1)) // n_chunks
        toks = [t for t in (fn() for fn in side[lo:hi]) if t is not None]
        if not toks:
            return None
        bits = pltpu.bitcast(functools.reduce(lambda a, b: a + b, toks), jnp.uint32)
        zero = lax.shift_right_logical(lax.shift_right_logical(bits, jnp.uint32(16)), jnp.uint32(16))
        return pltpu.bitcast(zero, F32).astype(BF16)

    acc = h_ref[...]
    k0 = 0
    for y_ref in y_refs:
        kw = y_ref.shape[1]
        acc = acc + _dot(y_ref[...], wo_ref[k0:k0 + kw, :])
        k0 += kw
    o_ref[...] = acc
    hn = _rms(acc, g_ref[...]).astype(BF16)
    for ci in range(n_chunks):
        f0 = ci * FFN_FC
        gt = _dot(hn, wg_ref[:, f0:f0 + FFN_FC])
        up = _dot(hn, wu_ref[:, f0:f0 + FFN_FC])
        a_scr[:, f0:f0 + FFN_FC] = (gt * _sigmoid(gt) * up).astype(BF16)
        zero = run_side(ci)
        if zero is not None:
            a_scr[0:BF16_ROWS, f0:f0 + LANES] = a_scr[0:BF16_ROWS, f0:f0 + LANES] + zero
    out = o_ref[...] + _dot(a_scr[...], wd_ref[...])
    if final:
        out = _rms(out, fg_ref[...])
    o_ref[...] = out
    y_s[...] = y_next[...]


def _out_ffn(h, side_kind, side_args, seq, w_out, g, layer, w_gate, w_up, w_down, final_g, final):
    t = h.shape[0]
    n = t // FFN_TM
    const = lambda i: (0, 0)
    resident = lambda shape: pl.BlockSpec(shape, const, pipeline_mode=pl.Buffered(1))
    slab = lambda shape: pl.BlockSpec((None,) + shape, lambda i: (layer, 0, 0),
                                      pipeline_mode=pl.Buffered(1))
    row = lambda i: (jnp.maximum(i - 1, 0), 0)
    ahead = lambda col: (lambda i: (jnp.minimum(i, n - 1), col))
    if side_kind == "conv":
        yb, main, conv_w, conv_b, ln_g, ln_b = side_args
        cw = jnp.concatenate([conv_w.reshape(CONV_WIDTH, CONV_CH), jnp.zeros((1, CONV_CH), F32)], axis=0)
        vec = lambda v: v.reshape(1, CONV_CH)
        y_width = CONV_CH
        side_specs = [pl.BlockSpec((FFN_TM, yb.shape[1]), row),
                      pl.BlockSpec((FFN_TM, 2 * CONV_CH), ahead(0)),
                      pl.BlockSpec((CONV_WIDTH + 1, CONV_CH), const),
                      pl.BlockSpec((1, CONV_CH), const),
                      pl.BlockSpec((1, CONV_CH), const),
                      pl.BlockSpec((1, CONV_CH), const)]
        side_ops = [yb, main, cw, vec(conv_b), vec(ln_g), vec(ln_b)]
        side_scratch = [pltpu.VMEM((CONV_HALO + CONV_TS, CONV_CH), F32),
                        pltpu.VMEM((SUBLANES - 1, CONV_TS + CONV_HALO - SUBLANES, CONV_CH), F32)]
    else:
        assert GLA_TB == FFN_TM
        main, glr, gate_w, gate_b, onorm_g = side_args
        nk = GLA_HEADS * GLA_DK
        nv = GLA_HEADS * GLA_DV
        gw = jnp.concatenate([gate_w, jnp.zeros((LANES - GLA_GATE_RANK, nk), F32)], axis=0)
        y_width = nv
        side_specs = [pl.BlockSpec((GLA_TB, nk), ahead(0)),
                      pl.BlockSpec((GLA_TB, nk), ahead(1)),
                      pl.BlockSpec((GLA_TB, nv), ahead(1)),
                      pl.BlockSpec((GLA_TB, nv), ahead(2)),
                      pl.BlockSpec((GLA_TB, LANES), ahead(0)),
                      pl.BlockSpec((LANES, nk), const),
                      pl.BlockSpec((1, nk), const),
                      pl.BlockSpec((1, nv), const)]
        side_ops = [main, main, main, main, glr, gw, gate_b.reshape(1, nk), onorm_g.reshape(1, nv)]
        side_scratch = [pltpu.VMEM((GLA_HEADS, GLA_DV, GLA_DK), F32),
                        pltpu.VMEM((GLA_TB, nk), F32),
                        pltpu.VMEM((GLA_TB, nk), BF16),
                        pltpu.VMEM((GLA_TB, nk), BF16),
                        pltpu.VMEM((GLA_TB, nk), BF16)]
    kern = functools.partial(_out_ffn_kernel, side_kind=side_kind, tiles_per_seq=seq // FFN_TM,
                             final=final)
    return pl.pallas_call(
        kern,
        grid=(n + 1,),
        in_specs=(
            [pl.BlockSpec((FFN_TM, D_MODEL), row)]
            + side_specs
            + [resident((w_out.shape[0], D_MODEL)),
               pl.BlockSpec((1, D_MODEL), const),
               slab((D_MODEL, D_FF)),
               slab((D_MODEL, D_FF)),
               slab((D_FF, D_MODEL)),
               pl.BlockSpec((1, D_MODEL), const)]),
        out_specs=pl.BlockSpec((FFN_TM, D_MODEL), row),
        out_shape=jax.ShapeDtypeStruct((t, D_MODEL), F32),
        scratch_shapes=[pltpu.VMEM((FFN_TM, D_FF), BF16),
                        pltpu.VMEM((FFN_TM, y_width), BF16),
                        pltpu.VMEM((FFN_TM, y_width), BF16)] + side_scratch,
        compiler_params=pltpu.CompilerParams(
            dimension_semantics=("arbitrary",), vmem_limit_bytes=VMEM_LIMIT),
    )(h, *side_ops, w_out.astype(BF16), g.reshape(1, D_MODEL), w_gate.astype(BF16),
      w_up.astype(BF16), w_down.astype(BF16), final_g.reshape(1, D_MODEL))


def kernel(x, norm_mix_g, ab_w_in, ab_conv_w, ab_conv_b, ab_ln_g, ab_ln_b, ab_w_out,
           c_w_in, c_gate_w, c_gate_b, c_onorm_g, c_w_out,
           norm_ffn_g, ffn_w_gate, ffn_w_up, ffn_w_down, final_norm_g):
    batch, seq, _ = x.shape
    t = batch * seq
    h = x.reshape(t, D_MODEL)
    nh = ATT_HEADS * HEAD_DIM

    w = ab_w_in[0]
    o_q = 2 * CONV_CH
    o_k = o_q + nh
    o_v = o_k + HEAD_DIM
    o_iq = o_v + HEAD_DIM
    o_ik = o_iq + IDX_HEADS * HEAD_DIM
    o_iw = o_ik + HEAD_DIM
    w_k, w_v, w_ik = w[:, o_k:o_v], w[:, o_v:o_iq], w[:, o_ik:o_iw]
    w_iw = jnp.concatenate([w[:, o_iw:], jnp.zeros((D_MODEL, LANES - IDX_HEADS), F32)], axis=1)
    w_all = jnp.concatenate(
        [w[:, :o_k], w[:, o_iq:o_ik], w_k, w_k, w_ik, w_ik, w_v, w_v, w_iw], axis=1).astype(BF16)
    n_main = 2 * CONV_CH + 2 * nh
    main, small = _norm_proj(h, norm_mix_g[0], w_all,
                             ((0, n_main), (n_main, n_main + 4 * LANES)), (BF16, F32))
    yb = _dsa(main, small, _rope_lane_tables(seq), batch, seq)
    h = _out_ffn(h, "conv", (yb, main, ab_conv_w[0], ab_conv_b[0], ab_ln_g[0], ab_ln_b[0]), seq,
                 ab_w_out[0], norm_ffn_g[0], 0, ffn_w_gate, ffn_w_up, ffn_w_down,
                 final_norm_g, False)

    w = c_w_in[0]
    n_main = 2 * GLA_HEADS * GLA_DK + 2 * GLA_HEADS * GLA_DV
    w_all = jnp.concatenate(
        [w, jnp.zeros((D_MODEL, LANES - GLA_GATE_RANK), F32)], axis=1).astype(BF16)
    main, glr = _norm_proj(h, norm_mix_g[1], w_all,
                           ((0, n_main), (n_main, n_main + LANES)), (BF16, F32))
    h = _out_ffn(h, "gla", (main, glr, c_gate_w[0], c_gate_b[0], c_onorm_g[0]), seq,
                 c_w_out[0], norm_ffn_g[1], 1, ffn_w_gate, ffn_w_up, ffn_w_down,
                 final_norm_g, True)
    return h.reshape(batch, seq, D_MODEL)
```

```python
import functools

import jax
import jax.numpy as jnp
from jax import lax
from jax.experimental import pallas as pl
from jax.experimental.pallas import tpu as pltpu

F32 = jnp.float32
BF16 = jnp.bfloat16

D_MODEL = 1024
CONV_CH = 512
CONV_WIDTH = 31
ATT_HEADS = 8
HEAD_DIM = 64
IDX_HEADS = 8
IDX_TOPK = 256
GLA_HEADS = 4
GLA_DK = 128
GLA_DV = 256
GLA_GATE_RANK = 16
GLA_TAU = 16.0
GLA_CHUNK = 64
ROPE_THETA = 500000.0
ROPE_ROT = HEAD_DIM // 4
D_FF = 2816
NORM_EPS = 1e-6
LOG2_E = 1.4426950408889634

LANES = 128
SUBLANES = 8
VMEM_LIMIT = 56 * 1024 * 1024

PROJ_TM = 1024
FFN_TM = 512
FFN_FC = 256
CONV_TS = 256
CONV_RC = 32
CONV_HALO = 32
DSA_TQ = 256
GLA_TB = 512
GLA_SUB = 256
BF16_ROWS = 16
F32_EPS = 2.0 ** -23
TOPK_LEVELS = 4
TOPK_MAX_FIX_ROUNDS = 400
TIE_CHUNK = 256
COUNT_ACCS = 4


def _sigmoid(x):
    return 1.0 / (1.0 + jnp.exp(-x))


def _rms(x, g):
    ms = jnp.mean(x * x, axis=-1, keepdims=True)
    return x * lax.rsqrt(ms + NORM_EPS) * g


def _dot(a, b):
    return jnp.dot(a, b, preferred_element_type=F32)


def _dot_nt(a, b):
    return lax.dot_general(a, b, (((1,), (1,)), ((), ())), preferred_element_type=F32)


def _dot_tn(a, b):
    return lax.dot_general(a, b, (((0,), (0,)), ((), ())), preferred_element_type=F32)


def _split3(x):
    hi = x.astype(BF16)
    r1 = x - hi.astype(F32)
    mid = r1.astype(BF16)
    lo = (r1 - mid.astype(F32)).astype(BF16)
    return hi, mid, lo


def _norm_proj_kernel(x_ref, g_ref, w_ref, *out_refs, groups, nchunk):
    xn = _rms(x_ref[...], g_ref[...]).astype(BF16)
    for (c0, c1), o_ref in zip(groups, out_refs):
        for n0 in range(c0, c1, nchunk):
            n1 = min(n0 + nchunk, c1)
            o_ref[:, n0 - c0:n1 - c0] = _dot(xn, w_ref[:, n0:n1]).astype(o_ref.dtype)


def _norm_proj(h, g, w, groups, dtypes):
    t = h.shape[0]
    n = w.shape[1]
    kern = functools.partial(_norm_proj_kernel, groups=groups, nchunk=512)
    return pl.pallas_call(
        kern,
        grid=(t // PROJ_TM,),
        in_specs=[
            pl.BlockSpec((PROJ_TM, D_MODEL), lambda i: (i, 0)),
            pl.BlockSpec((1, D_MODEL), lambda i: (0, 0)),
            pl.BlockSpec((D_MODEL, n), lambda i: (0, 0)),
        ],
        out_specs=[pl.BlockSpec((PROJ_TM, c1 - c0), lambda i: (i, 0)) for c0, c1 in groups],
        out_shape=[jax.ShapeDtypeStruct((t, c1 - c0), dt) for (c0, c1), dt in zip(groups, dtypes)],
        compiler_params=pltpu.CompilerParams(
            dimension_semantics=("arbitrary",), vmem_limit_bytes=VMEM_LIMIT),
    )(h, g.reshape(1, D_MODEL), w)


def _conv_stages(ua_ref, r_in, w_ref, cb, lg, lb, o_ref, r_out, hbuf, ysh, fresh):
    ts = CONV_TS
    base = CONV_HALO - (CONV_WIDTH - 1)

    def prepare():
        tail = hbuf[ts:ts + CONV_HALO, :]
        hbuf[0:CONV_HALO, :] = jnp.where(fresh, 0.0, tail)
        a = ua_ref[r_in:r_in + ts, 0:CONV_CH].astype(F32)
        gate = ua_ref[r_in:r_in + ts, CONV_CH:2 * CONV_CH].astype(F32)
        hbuf[CONV_HALO:CONV_HALO + ts, :] = a * _sigmoid(gate)
        ylen = ysh.shape[1]
        for b in range(1, SUBLANES):
            ysh[b - 1] = hbuf[b:b + ylen, :]
        return None

    def rows(r):
        acc = jnp.zeros((CONV_RC, CONV_CH), F32)
        for s in range(base, base + CONV_WIDTH):
            a8, b = divmod(s, SUBLANES)
            r8 = r + a8 * SUBLANES
            tap = hbuf[r8:r8 + CONV_RC, :] if b == 0 else ysh[b - 1, r8:r8 + CONV_RC, :]
            acc = acc + tap * w_ref[s - base:s - base + 1, :]
        acc = acc + cb
        mu = jnp.mean(acc, axis=-1, keepdims=True)
        d = acc - mu
        var = jnp.mean(d * d, axis=-1, keepdims=True)
        y = d * lax.rsqrt(var + NORM_EPS) * lg + lb
        o_ref[r_out + r:r_out + r + CONV_RC, :] = (y * _sigmoid(y)).astype(o_ref.dtype)
        return y[0:BF16_ROWS, 0:LANES]

    return [prepare] + [functools.partial(rows, r) for r in range(0, ts, CONV_RC)]


def _rope_group(xg, c, a, bm):
    return xg * c + pltpu.roll(xg, LANES - ROPE_ROT // 2, 1) * a + pltpu.roll(xg, ROPE_ROT // 2, 1) * bm


def _count_ge(x, c):
    return jnp.sum(jnp.where(x >= c, 1.0, 0.0), axis=0, keepdims=True)


def _count_ge_packed(db_ref, kv, cand):
    tq = db_ref.shape[1]
    cand_b = jnp.broadcast_to(cand, (BF16_ROWS, tq)).astype(BF16)
    one = jnp.ones((BF16_ROWS, tq), BF16)
    zero = jnp.zeros((BF16_ROWS, tq), BF16)
    accs = [None] * COUNT_ACCS
    for i, r in enumerate(range(0, kv, BF16_ROWS)):
        part = jnp.where(db_ref[r:r + BF16_ROWS, :] >= cand_b, one, zero)
        j = i % COUNT_ACCS
        accs[j] = part if accs[j] is None else accs[j] + part
    total = functools.reduce(lambda a, b: a + b, [a for a in accs if a is not None])
    return jnp.sum(total.astype(F32), axis=0, keepdims=True)


def _select_topk(sc_s, db_s, bt_s, kv, kf, causal, few):
    tq = sc_s.shape[1]
    neg_inf = -jnp.inf
    s_all = sc_s[0:kv, :]
    mx = jnp.max(s_all, axis=0, keepdims=True)
    mn = jnp.min(jnp.where(causal, s_all, jnp.inf), axis=0, keepdims=True)
    hi_top = mx + jnp.maximum((mx - mn) * (1.0 / 64.0), jnp.abs(mx) * (2.0 ** -20) + 1e-30)

    lo, hi = mn, hi_top
    for _ in range(TOPK_LEVELS):
        ulp = jnp.maximum(jnp.maximum(jnp.abs(lo), jnp.abs(hi)) * F32_EPS, 1e-30)
        step = jnp.maximum((hi - lo) * (1.0 / 256.0), ulp)
        inv = 1.0 / step
        db_s[0:kv, :] = jnp.clip((sc_s[0:kv, :] - lo) * inv, -1.0, 257.0).astype(BF16)

        def digit(_, carry):
            c, bit = carry
            cand = c + bit
            cnt = _count_ge_packed(db_s, kv, cand)
            return jnp.where(cnt >= kf, cand, c), bit * 0.5

        c, _ = lax.fori_loop(0, 8, digit, (jnp.zeros((1, tq), F32), jnp.full((1, tq), 128.0, F32)))
        lo, hi = lo + (c - 1.0) * step, lo + (c + 2.0) * step

    def verify(hi):
        s = sc_s[0:kv, :]
        below = s < hi
        vstar = jnp.max(jnp.where(below, s, neg_inf), axis=0, keepdims=True)
        c_hi = float(kv) - jnp.sum(jnp.where(below, 1.0, 0.0), axis=0, keepdims=True)
        return vstar, c_hi, _count_ge(s, vstar)

    def is_ok(c_hi, c_ge):
        ok = jnp.logical_and(c_hi < kf, c_ge >= kf)
        return ok if few is None else jnp.logical_or(ok, few)

    def unfinished(st):
        _, _, _, _, c_hi, c_ge, n = st
        bad = jnp.max(jnp.where(is_ok(c_hi, c_ge), 0.0, 1.0))
        return jnp.logical_and(bad > 0.5, n < TOPK_MAX_FIX_ROUNDS)

    def fix(st):
        lo, hi, span, vstar, c_hi, c_ge, n = st
        bad = jnp.logical_not(is_ok(c_hi, c_ge))
        hi_true = jnp.logical_and(bad, c_hi >= kf)
        lo = jnp.where(hi_true, hi, lo)
        hi = jnp.where(hi_true, jnp.minimum(hi + span, hi_top),
                       jnp.where(jnp.logical_and(bad, c_ge < kf), vstar, hi))
        span = jnp.where(hi_true, span * 8.0, span)
        mid = 0.5 * lo + 0.5 * hi
        up = _count_ge(sc_s[0:kv, :], mid) >= kf
        lo = jnp.where(jnp.logical_and(bad, up), mid, lo)
        hi = jnp.where(jnp.logical_and(bad, jnp.logical_not(up)), mid, hi)
        vstar, c_hi, c_ge = verify(hi)
        return lo, hi, span, vstar, c_hi, c_ge, n + 1

    vstar, c_hi, c_ge = verify(hi)
    _, _, _, vstar, c_hi, c_ge, _ = lax.while_loop(
        unfinished, fix, (mn, hi, hi - lo, vstar, c_hi, c_ge, jnp.int32(0)))

    need = kf - c_hi
    if few is not None:
        vstar = jnp.where(few, neg_inf, vstar)
        c_ge = jnp.where(few, 0.0, c_ge)
    has_extra_ties = jnp.max(c_ge) > kf

    @pl.when(jnp.logical_not(has_extra_ties))
    def _():
        keep = sc_s[0:kv, :] >= vstar
        if few is not None:
            keep = jnp.logical_and(keep, causal)
        bt_s[0:kv, :] = jnp.where(keep, 0.0, neg_inf)

    @pl.when(has_extra_ties)
    def _():
        ri = lax.broadcasted_iota(jnp.int32, (TIE_CHUNK, TIE_CHUNK), 0)
        ci = lax.broadcasted_iota(jnp.int32, (TIE_CHUNK, TIE_CHUNK), 1)
        tri = jnp.where(ci <= ri, 1.0, 0.0).astype(BF16)
        seen = jnp.zeros((1, tq), F32)
        for c0 in range(0, kv, TIE_CHUNK):
            s = sc_s[c0:c0 + TIE_CHUNK, :]
            eq = jnp.where(s == vstar, 1.0, 0.0)
            rank = _dot(tri, eq.astype(BF16)) + seen
            seen = seen + jnp.sum(eq, axis=0, keepdims=True)
            keep = jnp.logical_or(s > vstar, jnp.logical_and(s == vstar, rank <= need))
            if few is not None:
                keep = jnp.logical_and(keep, causal[c0:c0 + TIE_CHUNK, :])
            bt_s[c0:c0 + TIE_CHUNK, :] = jnp.where(keep, 0.0, neg_inf)


def _dsa_kernel(q_ref, iq_ref, small_ref, tc_ref, ta_ref, tb_ref, o_ref,
                k_s, ik_s, v_s, sc_s, db_s, bt_s, bias_s, *, seq):
    tq = DSA_TQ
    w_scale = (IDX_HEADS ** -0.5) * (HEAD_DIM ** -0.5)
    att_scale = HEAD_DIM ** -0.5
    n_sel = min(IDX_TOPK, seq // 4)
    lane = lax.broadcasted_iota(jnp.int32, (1, LANES), 1)
    lo_half = lane < HEAD_DIM

    tc = tc_ref[...]
    ta = ta_ref[...]
    tb = tb_ref[...]
    k_s[...] = _rope_group(small_ref[:, 0:LANES], tc, ta, tb).astype(BF16)
    ik_s[...] = _rope_group(small_ref[:, LANES:2 * LANES], tc, ta, tb).astype(BF16)
    vv = small_ref[:, 2 * LANES:3 * LANES]
    v_s[0] = jnp.where(lo_half, vv, 1.0).astype(BF16)
    v_s[1] = jnp.where(lo_half, 1.0, vv).astype(BF16)

    def rope_split(ref, r0, g, scale=None):
        xg = ref[r0:r0 + tq, g * LANES:(g + 1) * LANES].astype(F32)
        xr = _rope_group(xg, tc_ref[r0:r0 + tq, :], ta_ref[r0:r0 + tq, :], tb_ref[r0:r0 + tq, :])
        if scale is not None:
            xr = xr * scale
        return (jnp.where(lo_half, xr, 0.0).astype(BF16), jnp.where(lo_half, 0.0, xr).astype(BF16))

    for blk in range(seq // tq):
        r0 = blk * tq
        kv = r0 + tq
        iw_t = jnp.transpose(small_ref[r0:r0 + tq, 3 * LANES:4 * LANES])
        ik = ik_s[0:kv, :]
        score = jnp.zeros((kv, tq), F32)
        for g in range(IDX_HEADS // 2):
            halves = rope_split(iq_ref, r0, g)
            for par in range(2):
                hh = 2 * g + par
                logits = _dot_nt(ik, halves[par])
                score = score + jnp.maximum(logits, 0.0) * iw_t[hh:hh + 1, :]
        score = score * w_scale
        key_pos = lax.broadcasted_iota(jnp.int32, (kv, tq), 0)
        q_pos = r0 + lax.broadcasted_iota(jnp.int32, (kv, tq), 1)
        causal = key_pos <= q_pos
        sc_s[0:kv, :] = jnp.where(causal, score, -jnp.inf)

        if kv <= n_sel:
            bt_s[0:kv, :] = jnp.where(
                jnp.logical_and(causal, sc_s[0:kv, :] >= -jnp.inf), 0.0, -jnp.inf)
        else:
            few = None
            if r0 < n_sel:
                few = r0 + lax.broadcasted_iota(jnp.int32, (1, tq), 1) < n_sel
            _select_topk(sc_s, db_s, bt_s, kv, float(n_sel), causal, few)
        bias_s[:, 0:kv] = jnp.transpose(bt_s[0:kv, :])

        kk = k_s[0:kv, :]
        for g in range(ATT_HEADS // 2):
            halves = rope_split(q_ref, r0, g, att_scale * LOG2_E)
            outs = []
            for par in range(2):
                s = _dot_nt(halves[par], kk) + bias_s[:, 0:kv]
                m = jnp.max(s, axis=1, keepdims=True)
                p = jnp.exp2(s - m)
                o = _dot(p.astype(BF16), v_s[par, 0:kv, :])
                outs.append(o / pltpu.roll(o, HEAD_DIM, 1))
            o_ref[r0:r0 + tq, g * LANES:(g + 1) * LANES] = (
                jnp.where(lo_half, outs[0], outs[1]).astype(o_ref.dtype))


def _dsa(main, small, tabs, batch, seq):
    t = main.shape[0]
    nh = ATT_HEADS * HEAD_DIM
    const = lambda b: (0, 0)
    kern = functools.partial(_dsa_kernel, seq=seq)
    return pl.pallas_call(
        kern,
        grid=(batch,),
        in_specs=[
            pl.BlockSpec((seq, nh), lambda b: (b, 2)),
            pl.BlockSpec((seq, nh), lambda b: (b, 3)),
            pl.BlockSpec((seq, 4 * LANES), lambda b: (b, 0)),
            pl.BlockSpec((seq, LANES), const),
            pl.BlockSpec((seq, LANES), const),
            pl.BlockSpec((seq, LANES), const),
        ],
        out_specs=pl.BlockSpec((seq, nh), lambda b: (b, 0)),
        out_shape=jax.ShapeDtypeStruct((t, nh), BF16),
        scratch_shapes=[
            pltpu.VMEM((seq, LANES), BF16),
            pltpu.VMEM((seq, LANES), BF16),
            pltpu.VMEM((2, seq, LANES), BF16),
            pltpu.VMEM((seq, DSA_TQ), F32),
            pltpu.VMEM((seq, DSA_TQ), BF16),
            pltpu.VMEM((seq, DSA_TQ), F32),
            pltpu.VMEM((DSA_TQ, seq), F32),
        ],
        compiler_params=pltpu.CompilerParams(
            dimension_semantics=("arbitrary",), vmem_limit_bytes=VMEM_LIMIT),
    )(main, main, small, *tabs)


def _rope_lane_tables(seq):
    half = ROPE_ROT // 2
    inv = ROPE_THETA ** (-jnp.arange(0, ROPE_ROT, 2, dtype=F32) / ROPE_ROT)
    ang = jnp.arange(seq, dtype=F32)[:, None] * inv[None, :]
    cos, sin = jnp.cos(ang), jnp.sin(ang)
    rest = HEAD_DIM - ROPE_ROT
    c64 = jnp.concatenate([cos, cos, jnp.ones((seq, rest), F32)], axis=1)
    a64 = jnp.concatenate([-sin, jnp.zeros((seq, HEAD_DIM - half), F32)], axis=1)
    b64 = jnp.concatenate([jnp.zeros((seq, half), F32), sin, jnp.zeros((seq, rest), F32)], axis=1)
    return tuple(jnp.tile(x, (1, LANES // HEAD_DIM)) for x in (c64, a64, b64))


def _gla_stages(q_ref, k_ref, v_ref, r_ref, glr_ref, gw_ref, gb_ref, og_ref, o_ref,
                state, bc_s, qt_s, kt_s, ksc_s, fresh):
    tb, c, sb = GLA_TB, GLA_CHUNK, GLA_SUB
    tok = lambda x: x[0:BF16_ROWS, 0:LANES].astype(F32)

    ri = lax.broadcasted_iota(jnp.int32, (sb, sb), 0)
    ci = lax.broadcasted_iota(jnp.int32, (sb, sb), 1)
    shift = c.bit_length() - 1
    tri = jnp.logical_and(ci <= ri, lax.shift_right_logical(ri, shift) == lax.shift_right_logical(ci, shift))
    tri_b = jnp.where(tri, 1.0, 0.0).astype(BF16)
    gw_h, gw_m, gw_l = _split3(gw_ref[...])
    gb = gb_ref[...]
    og = og_ref[...]
    cmask = (lax.broadcasted_iota(jnp.int32, (c, c), 1) <= lax.broadcasted_iota(jnp.int32, (c, c), 0))

    def decays(s0):
        rows = slice(s0, s0 + sb)
        gl_h, gl_m, gl_l = _split3(glr_ref[rows, :])
        pre = (_dot(gl_h, gw_h) + (_dot(gl_h, gw_m) + _dot(gl_m, gw_h))
               + (_dot(gl_h, gw_l) + _dot(gl_m, gw_m) + _dot(gl_l, gw_h))) + gb
        log_a = (jnp.minimum(pre, 0.0) - jnp.log(1.0 + jnp.exp(-jnp.abs(pre)))) / GLA_TAU
        la_h, la_m, la_l = _split3(log_a)
        bcum = _dot(tri_b, la_h) + _dot(tri_b, la_m) + _dot(tri_b, la_l)
        bc_s[rows, :] = bcum
        qt = q_ref[rows, :].astype(F32) * (GLA_DK ** -0.5) * jnp.exp(bcum)
        qt_s[rows, :] = qt.astype(BF16)
        kt_s[rows, :] = (k_ref[rows, :].astype(F32) * jnp.exp(-bcum)).astype(BF16)
        return tok(qt)

    def scaled_keys(t0):
        rows = slice(t0, t0 + c)
        b_last = bc_s[t0 + c - 1:t0 + c, :]
        ksc = k_ref[rows, :].astype(F32) * jnp.exp(b_last - bc_s[rows, :])
        ksc_s[rows, :] = ksc.astype(BF16)
        return tok(ksc)

    carried = {}

    def head_chunk(hd, t0):
        ks = slice(hd * GLA_DK, (hd + 1) * GLA_DK)
        vs = slice(hd * GLA_DV, (hd + 1) * GLA_DV)
        rows = slice(t0, t0 + c)
        st = jnp.where(fresh, 0.0, state[hd]) if t0 == 0 else carried[hd]
        qh = qt_s[rows, ks]
        vh = v_ref[rows, vs]
        att = jnp.where(cmask, _dot_nt(qh, kt_s[rows, ks]), 0.0)
        o = _dot(att.astype(BF16), vh) + _dot_nt(qh, st.astype(BF16))
        st = st * jnp.exp(bc_s[t0 + c - 1:t0 + c, ks]) + _dot_tn(vh, ksc_s[rows, ks])
        o = o * lax.rsqrt(jnp.mean(o * o, axis=-1, keepdims=True) + NORM_EPS) * og[:, vs]
        rr = r_ref[rows, vs].astype(F32)
        o = o * (rr * _sigmoid(rr))
        o_ref[rows, vs] = o.astype(o_ref.dtype)
        if t0 + c == tb:
            state[hd] = st
        else:
            carried[hd] = st
        return tok(o)

    steps = [functools.partial(decays, s0) for s0 in range(0, tb, sb)]
    steps += [functools.partial(scaled_keys, t0) for t0 in range(0, tb, c)]
    steps += [functools.partial(head_chunk, hd, t0) for t0 in range(0, tb, c) for hd in range(GLA_HEADS)]
    return steps


def _out_ffn_kernel(*refs, side_kind, tiles_per_seq, final):
    step = pl.program_id(0)
    seq_start = lax.rem(step, tiles_per_seq) == 0
    h_ref = refs[0]
    if side_kind == "conv":
        (yb_ref, ua_ref, cw_ref, cb_ref, lg_ref, lb_ref, wo_ref, g_ref, wg_ref, wu_ref, wd_ref, fg_ref,
         o_ref, a_scr, y_s, y_next, hbuf, ysh) = refs[1:]
        carry = hbuf
        y_refs = [y_s, yb_ref]
    else:
        (q_ref, k_ref, v_ref, r_ref, glr_ref, gw_ref, gb_ref, og_ref, wo_ref, g_ref, wg_ref, wu_ref,
         wd_ref, fg_ref, o_ref, a_scr, y_s, y_next, state, bc_s, qt_s, kt_s, ksc_s) = refs[1:]
        carry = state
        y_refs = [y_s]

    @pl.when(step == 0)
    def _():
        carry[...] = jnp.zeros(carry.shape, F32)
        y_s[...] = jnp.zeros(y_s.shape, BF16)

    if side_kind == "conv":
        cb, lg, lb = cb_ref[...], lg_ref[...], lb_ref[...]
        side = []
        for r in range(0, FFN_TM, CONV_TS):
            fresh = seq_start if r == 0 else False
            side += _conv_stages(ua_ref, r, cw_ref, cb, lg, lb, y_next, r, hbuf, ysh, fresh)
    else:
        side = _gla_stages(q_ref, k_ref, v_ref, r_ref, glr_ref, gw_ref, gb_ref, og_ref, y_next,
                           state, bc_s, qt_s, kt_s, ksc_s, seq_start)
    n_chunks = D_FF // FFN_FC

    def run_side(slot):
        lo, hi = (len(side) * slot) // n_chunks, (len(side) * (slot + 1)) // n_chunks
        toks = [t for t in (fn() for fn in side[lo:hi]) if t is not None]
        if not toks:
            return None
        bits = pltpu.bitcast(functools.reduce(lambda a, b: a + b, toks), jnp.uint32)
        zero = lax.shift_right_logical(lax.shift_right_logical(bits, jnp.uint32(16)), jnp.uint32(16))
        return pltpu.bitcast(zero, F32).astype(BF16)

    acc = h_ref[...]
    k0 = 0
    for y_ref in y_refs:
        kw = y_ref.shape[1]
        acc = acc + _dot(y_ref[...], wo_ref[k0:k0 + kw, :])
        k0 += kw
    o_ref[...] = acc
    hn = _rms(acc, g_ref[...]).astype(BF16)
    for ci in range(n_chunks):
        f0 = ci * FFN_FC
        gt = _dot(hn, wg_ref[:, f0:f0 + FFN_FC])
        up = _dot(hn, wu_ref[:, f0:f0 + FFN_FC])
        a_scr[:, f0:f0 + FFN_FC] = (gt * _sigmoid(gt) * up).astype(BF16)
        zero = run_side(ci)
        if zero is not None:
            a_scr[0:BF16_ROWS, f0:f0 + LANES] = a_scr[0:BF16_ROWS, f0:f0 + LANES] + zero
    out = o_ref[...] + _dot(a_scr[...], wd_ref[...])
    if final:
        out = _rms(out, fg_ref[...])
    o_ref[...] = out
    y_s[...] = y_next[...]


def _out_ffn(h, side_kind, side_args, seq, w_out, g, layer, w_gate, w_up, w_down, final_g, final):
    t = h.shape[0]
    n = t // FFN_TM
    const = lambda i: (0, 0)
    resident = lambda shape: pl.BlockSpec(shape, const, pipeline_mode=pl.Buffered(1))
    slab = lambda shape: pl.BlockSpec((None,) + shape, lambda i: (layer, 0, 0),
                                      pipeline_mode=pl.Buffered(1))
    row = lambda i: (jnp.maximum(i - 1, 0), 0)
    ahead = lambda col: (lambda i: (jnp.minimum(i, n - 1), col))
    if side_kind == "conv":
        yb, main, conv_w, conv_b, ln_g, ln_b = side_args
        cw = jnp.concatenate([conv_w.reshape(CONV_WIDTH, CONV_CH), jnp.zeros((1, CONV_CH), F32)], axis=0)
        vec = lambda v: v.reshape(1, CONV_CH)
        y_width = CONV_CH
        side_specs = [pl.BlockSpec((FFN_TM, yb.shape[1]), row),
                      pl.BlockSpec((FFN_TM, 2 * CONV_CH), ahead(0)),
                      pl.BlockSpec((CONV_WIDTH + 1, CONV_CH), const),
                      pl.BlockSpec((1, CONV_CH), const),
                      pl.BlockSpec((1, CONV_CH), const),
                      pl.BlockSpec((1, CONV_CH), const)]
        side_ops = [yb, main, cw, vec(conv_b), vec(ln_g), vec(ln_b)]
        side_scratch = [pltpu.VMEM((CONV_HALO + CONV_TS, CONV_CH), F32),
                        pltpu.VMEM((SUBLANES - 1, CONV_TS + CONV_HALO - SUBLANES, CONV_CH), F32)]
    else:
        assert GLA_TB == FFN_TM
        main, glr, gate_w, gate_b, onorm_g = side_args
        nk = GLA_HEADS * GLA_DK
        nv = GLA_HEADS * GLA_DV
        gw = jnp.concatenate([gate_w, jnp.zeros((LANES - GLA_GATE_RANK, nk), F32)], axis=0)
        y_width = nv
        side_specs = [pl.BlockSpec((GLA_TB, nk), ahead(0)),
                      pl.BlockSpec((GLA_TB, nk), ahead(1)),
                      pl.BlockSpec((GLA_TB, nv), ahead(1)),
                      pl.BlockSpec((GLA_TB, nv), ahead(2)),
                      pl.BlockSpec((GLA_TB, LANES), ahead(0)),
                      pl.BlockSpec((LANES, nk), const),
                      pl.BlockSpec((1, nk), const),
                      pl.BlockSpec((1, nv), const)]
        side_ops = [main, main, main, main, glr, gw, gate_b.reshape(1, nk), onorm_g.reshape(1, nv)]
        side_scratch = [pltpu.VMEM((GLA_HEADS, GLA_DV, GLA_DK), F32),
                        pltpu.VMEM((GLA_TB, nk), F32),
                        pltpu.VMEM((GLA_TB, nk), BF16),
                        pltpu.VMEM((GLA_TB, nk), BF16),
                        pltpu.VMEM((GLA_TB, nk), BF16)]
    kern = functools.partial(_out_ffn_kernel, side_kind=side_kind, tiles_per_seq=seq // FFN_TM,
                             final=final)
    return pl.pallas_call(
        kern,
        grid=(n + 1,),
        in_specs=(
            [pl.BlockSpec((FFN_TM, D_MODEL), row)]
            + side_specs
            + [resident((w_out.shape[0], D_MODEL)),
               pl.BlockSpec((1, D_MODEL), const),
               slab((D_MODEL, D_FF)),
               slab((D_MODEL, D_FF)),
               slab((D_FF, D_MODEL)),
               pl.BlockSpec((1, D_MODEL), const)]),
        out_specs=pl.BlockSpec((FFN_TM, D_MODEL), row),
        out_shape=jax.ShapeDtypeStruct((t, D_MODEL), F32),
        scratch_shapes=[pltpu.VMEM((FFN_TM, D_FF), BF16),
                        pltpu.VMEM((FFN_TM, y_width), BF16),
                        pltpu.VMEM((FFN_TM, y_width), BF16)] + side_scratch,
        compiler_params=pltpu.CompilerParams(
            dimension_semantics=("arbitrary",), vmem_limit_bytes=VMEM_LIMIT),
    )(h, *side_ops, w_out.astype(BF16), g.reshape(1, D_MODEL), w_gate.astype(BF16),
      w_up.astype(BF16), w_down.astype(BF16), final_g.reshape(1, D_MODEL))


def kernel(x, norm_mix_g, ab_w_in, ab_conv_w, ab_conv_b, ab_ln_g, ab_ln_b, ab_w_out,
           c_w_in, c_gate_w, c_gate_b, c_onorm_g, c_w_out,
           norm_ffn_g, ffn_w_gate, ffn_w_up, ffn_w_down, final_norm_g):
    batch, seq, _ = x.shape
    t = batch * seq
    h = x.reshape(t, D_MODEL)
    nh = ATT_HEADS * HEAD_DIM

    w = ab_w_in[0]
    o_q = 2 * CONV_CH
    o_k = o_q + nh
    o_v = o_k + HEAD_DIM
    o_iq = o_v + HEAD_DIM
    o_ik = o_iq + IDX_HEADS * HEAD_DIM
    o_iw = o_ik + HEAD_DIM
    w_k, w_v, w_ik = w[:, o_k:o_v], w[:, o_v:o_iq], w[:, o_ik:o_iw]
    w_iw = jnp.concatenate([w[:, o_iw:], jnp.zeros((D_MODEL, LANES - IDX_HEADS), F32)], axis=1)
    w_all = jnp.concatenate(
        [w[:, :o_k], w[:, o_iq:o_ik], w_k, w_k, w_ik, w_ik, w_v, w_v, w_iw], axis=1).astype(BF16)
    n_main = 2 * CONV_CH + 2 * nh
    main, small = _norm_proj(h, norm_mix_g[0], w_all,
                             ((0, n_main), (n_main, n_main + 4 * LANES)), (BF16, F32))
    yb = _dsa(main, small, _rope_lane_tables(seq), batch, seq)
    h = _out_ffn(h, "conv", (yb, main, ab_conv_w[0], ab_conv_b[0], ab_ln_g[0], ab_ln_b[0]), seq,
                 ab_w_out[0], norm_ffn_g[0], 0, ffn_w_gate, ffn_w_up, ffn_w_down,
                 final_norm_g, False)

    w = c_w_in[0]
    n_main = 2 * GLA_HEADS * GLA_DK + 2 * GLA_HEADS * GLA_DV
    w_all = jnp.concatenate(
        [w, jnp.zeros((D_MODEL, LANES - GLA_GATE_RANK), F32)], axis=1).astype(BF16)
    main, glr = _norm_proj(h, norm_mix_g[1], w_all,
                           ((0, n_main), (n_main, n_main + LANES)), (BF16, F32))
    h = _out_ffn(h, "gla", (main, glr, c_gate_w[0], c_gate_b[0], c_onorm_g[0]), seq,
                 c_w_out[0], norm_ffn_g[1], 1, ffn_w_gate, ffn_w_up, ffn_w_down,
                 final_norm_g, True)
    return h.reshape(batch, seq, D_MODEL)
```
